```python
import jax, jax.numpy as jnp
from jax import lax
import numpy as np

D_MODEL = 4096
BATCH = 2
SEQ = 8192
DEPTH = 1

D_FF = 11008
PLE_DIM = 256
NORM_EPS = 1e-6
SSM_HEAD_DIM = 64
SSM_D_INNER = D_MODEL
SSM_HEADS = SSM_D_INNER // SSM_HEAD_DIM
SSM_GROUPS = 8
SSM_D_STATE = 128
SSM_CONV = 4
SSM_CHUNK = 128
SSM_CONV_DIM = SSM_D_INNER + 2 * SSM_GROUPS * SSM_D_STATE
SB_HEAD_DIM = 128
SB_HEADS = D_MODEL // SB_HEAD_DIM
SB_WIDTH = SB_HEADS * SB_HEAD_DIM
SB_BLOCK = 128
OFF_Z = 0
OFF_XBC = OFF_Z + SSM_D_INNER
OFF_DT = OFF_XBC + SSM_CONV_DIM
OFF_Q = OFF_DT + SSM_HEADS
OFF_K = OFF_Q + SB_WIDTH
OFF_V = OFF_K + SB_WIDTH
OFF_GATE = OFF_V + SB_WIDTH
IN_COLS = OFF_GATE + 2 * D_MODEL

kernel_name = "hybrid_ssd_stickbreak_macaron_layer"


def rms_norm(x, w):
    xf = x.astype(jnp.float32)
    y = xf * lax.rsqrt(jnp.mean(xf * xf, axis=-1, keepdims=True) + NORM_EPS)
    return (y * w.astype(jnp.float32)).astype(x.dtype)


def swiglu(x, w1, w3, w2):
    return (jax.nn.silu(x @ w1) * (x @ w3)) @ w2


def causal_depthwise_conv(x, w, b):
    c = x.shape[-1]
    out = lax.conv_general_dilated(
        x, w[:, None, :].astype(x.dtype), window_strides=(1,),
        padding=[(w.shape[0] - 1, 0)],
        dimension_numbers=("NWC", "WIO", "NWC"), feature_group_count=c)
    return out + b.astype(x.dtype)


def segsum_decay(a):
    t = a.shape[-1]
    a_rep = jnp.broadcast_to(a[..., :, None], a.shape + (t,))
    strict = jnp.tril(jnp.ones((t, t), dtype=bool), -1)
    incl = jnp.tril(jnp.ones((t, t), dtype=bool), 0)
    cs = jnp.cumsum(jnp.where(strict, a_rep, 0.0), axis=-2)
    return jnp.where(incl, jnp.exp(jnp.where(incl, cs, 0.0)), 0.0)


def ssd_mixer(z, xbc, dt_raw, conv_w, conv_b, dt_bias, a_log, d_skip, norm_w):
    bsz, seqlen, _ = z.shape
    nc = seqlen // SSM_CHUNK
    g, r, p, n, l = SSM_GROUPS, SSM_HEADS // SSM_GROUPS, SSM_HEAD_DIM, SSM_D_STATE, SSM_CHUNK
    xbc = jax.nn.silu(causal_depthwise_conv(xbc, conv_w, conv_b)).astype(jnp.float32)
    xs = xbc[..., :SSM_D_INNER].reshape(bsz, seqlen, SSM_HEADS, p)
    bm = xbc[..., SSM_D_INNER:SSM_D_INNER + g * n].reshape(bsz, nc, l, g, n)
    cm = xbc[..., SSM_D_INNER + g * n:].reshape(bsz, nc, l, g, n)
    dt = jax.nn.softplus(dt_raw.astype(jnp.float32) + dt_bias.astype(jnp.float32))
    a = -jnp.exp(a_log.astype(jnp.float32))
    xdt = (xs * dt[..., None]).reshape(bsz, nc, l, g, r, p)
    da = jnp.transpose((dt * a).reshape(bsz, nc, l, SSM_HEADS), (0, 3, 1, 2))
    a_cum = jnp.cumsum(da, axis=-1)
    lmat = segsum_decay(da).reshape(bsz, g, r, nc, l, l)
    cb = jnp.einsum('bclgn,bcsgn->bcgls', cm, bm)
    y_diag = jnp.einsum('bcgls,bgrcls,bcsgrp->bclgrp', cb, lmat, xdt)
    decay_states = jnp.exp(a_cum[..., -1:] - a_cum).reshape(bsz, g, r, nc, l)
    states = jnp.einsum('bclgn,bgrcl,bclgrp->bcgrpn', bm, decay_states, xdt)
    states = jnp.moveaxis(states.reshape(bsz, nc, SSM_HEADS, p, n), 1, 0)
    chunk_decay = jnp.moveaxis(jnp.exp(a_cum[..., -1]), 2, 0)

    def step(h, inp):
        s_c, d_c = inp
        return h * d_c[..., None, None] + s_c, h

    _, prev = lax.scan(step, jnp.zeros((bsz, SSM_HEADS, p, n), jnp.float32), (states, chunk_decay))
    prev = jnp.moveaxis(prev, 0, 1).reshape(bsz, nc, g, r, p, n)
    out_decay = jnp.exp(a_cum).reshape(bsz, g, r, nc, l)
    y_off = jnp.einsum('bclgn,bcgrpn,bgrcl->bclgrp', cm, prev, out_decay)
    y = (y_diag + y_off).reshape(bsz, seqlen, SSM_HEADS, p) + d_skip.astype(jnp.float32)[:, None] * xs
    y = y.reshape(bsz, seqlen, SSM_D_INNER) * jax.nn.silu(z.astype(jnp.float32))
    return rms_norm(y, norm_w).astype(z.dtype)


def stick_breaking_attention(q, k, v):
    bsz, nh, seqlen, dh = q.shape
    nb = seqlen // SB_BLOCK
    scale = 1.0 / np.sqrt(dh).astype(np.float32)
    kf = k.astype(jnp.float32)
    vf = v.astype(jnp.float32)
    s_idx = jnp.arange(seqlen)

    def block(i):
        q_blk = lax.dynamic_slice_in_dim(q, i * SB_BLOCK, SB_BLOCK, axis=2).astype(jnp.float32)
        logits = jnp.einsum('bhtd,bhsd->bhts', q_blk, kf) * scale
        t_idx = i * SB_BLOCK + jnp.arange(SB_BLOCK)
        causal = s_idx[None, :] < t_idx[:, None]
        log_beta = jax.nn.log_sigmoid(logits)
        log_1mb = jnp.where(causal, jax.nn.log_sigmoid(-logits), 0.0)
        rev = lax.cumsum(log_1mb, axis=3, reverse=True)
        excl = jnp.concatenate([rev[..., 1:], jnp.zeros_like(rev[..., :1])], axis=-1)
        attn = jnp.where(causal, jnp.exp(log_beta + excl), 0.0)
        return jnp.einsum('bhts,bhsd->bhtd', attn, vf)

    out = lax.map(block, jnp.arange(nb))
    out = jnp.transpose(out, (1, 0, 3, 2, 4)).reshape(bsz, seqlen, nh * dh)
    return out.astype(q.dtype)


def setup_inputs(seed: int = 0) -> dict:
    key = jax.random.key(seed)
    ks = jax.random.split(key, 32)
    f32 = jnp.float32

    def dense(k, shape, fan_in):
        return jax.random.normal(k, shape, f32) * (fan_in ** -0.5)

    def gain(k, shape):
        return 1.0 + 0.02 * jax.random.normal(k, shape, f32)

    dt0 = jnp.exp(jax.random.uniform(ks[9], (DEPTH, SSM_HEADS), f32, np.log(1e-3), np.log(1e-1)))
    return {
        "x": jax.random.normal(ks[0], (BATCH, SEQ, D_MODEL), f32),
        "p": jax.random.normal(ks[1], (DEPTH, BATCH, SEQ, PLE_DIM), f32),
        "ffn1_pre_w": gain(ks[2], (DEPTH, D_MODEL)),
        "ffn1_w1": dense(ks[3], (DEPTH, D_MODEL, D_FF), D_MODEL),
        "ffn1_w3": dense(ks[4], (DEPTH, D_MODEL, D_FF), D_MODEL),
        "ffn1_w2": dense(ks[5], (DEPTH, D_FF, D_MODEL), D_FF),
        "ffn1_post_w": gain(ks[6], (DEPTH, D_MODEL)),
        "mix_pre_w": gain(ks[7], (DEPTH, D_MODEL)),
        "w_in": dense(ks[8], (DEPTH, D_MODEL, IN_COLS), D_MODEL),
        "conv_w": 0.5 * jax.random.normal(ks[10], (DEPTH, SSM_CONV, SSM_CONV_DIM), f32),
        "conv_b": 0.01 * jax.random.normal(ks[11], (DEPTH, SSM_CONV_DIM), f32),
        "dt_bias": dt0 + jnp.log(-jnp.expm1(-dt0)),
        "a_log": jnp.log(jax.random.uniform(ks[12], (DEPTH, SSM_HEADS), f32, 1.0, 16.0)),
        "d_skip": 1.0 + 0.1 * jax.random.normal(ks[13], (DEPTH, SSM_HEADS), f32),
        "ssm_norm_w": gain(ks[14], (DEPTH, SSM_D_INNER)),
        "w_ssm_out": dense(ks[15], (DEPTH, SSM_D_INNER, D_MODEL), SSM_D_INNER),
        "w_sb_out": dense(ks[16], (DEPTH, SB_WIDTH, D_MODEL), SB_WIDTH),
        "w_out": dense(ks[17], (DEPTH, D_MODEL, D_MODEL), D_MODEL),
        "mix_post_w": gain(ks[18], (DEPTH, D_MODEL)),
        "ffn2_pre_w": gain(ks[19], (DEPTH, D_MODEL)),
        "ffn2_w1": dense(ks[20], (DEPTH, D_MODEL, D_FF), D_MODEL),
        "ffn2_w3": dense(ks[21], (DEPTH, D_MODEL, D_FF), D_MODEL),
        "ffn2_w2": dense(ks[22], (DEPTH, D_FF, D_MODEL), D_FF),
        "ffn2_post_w": gain(ks[23], (DEPTH, D_MODEL)),
        "ple_w_gate": dense(ks[24], (DEPTH, D_MODEL, D_MODEL), D_MODEL),
        "ple_w_proj": dense(ks[25], (DEPTH, PLE_DIM, D_MODEL), PLE_DIM),
        "ple_norm_w": gain(ks[26], (DEPTH, D_MODEL)),
    }


def reference(x, p, ffn1_pre_w, ffn1_w1, ffn1_w3, ffn1_w2, ffn1_post_w, mix_pre_w, w_in,
              conv_w, conv_b, dt_bias, a_log, d_skip, ssm_norm_w, w_ssm_out, w_sb_out, w_out,
              mix_post_w, ffn2_pre_w, ffn2_w1, ffn2_w3, ffn2_w2, ffn2_post_w,
              ple_w_gate, ple_w_proj, ple_norm_w):
    bsz, seqlen, _ = x.shape
    h = x
    for i in range(DEPTH):
        f = swiglu(rms_norm(h, ffn1_pre_w[i]), ffn1_w1[i], ffn1_w3[i], ffn1_w2[i])
        h = h + 0.5 * rms_norm(f, ffn1_post_w[i])

        u = rms_norm(h, mix_pre_w[i])
        proj = u @ w_in[i]
        z = proj[..., OFF_Z:OFF_XBC]
        xbc = proj[..., OFF_XBC:OFF_DT]
        dt_raw = proj[..., OFF_DT:OFF_Q]
        q = proj[..., OFF_Q:OFF_K].reshape(bsz, seqlen, SB_HEADS, SB_HEAD_DIM).transpose(0, 2, 1, 3)
        k = proj[..., OFF_K:OFF_V].reshape(bsz, seqlen, SB_HEADS, SB_HEAD_DIM).transpose(0, 2, 1, 3)
        v = proj[..., OFF_V:OFF_GATE].reshape(bsz, seqlen, SB_HEADS, SB_HEAD_DIM).transpose(0, 2, 1, 3)
        gate_ssm = jax.nn.sigmoid(proj[..., OFF_GATE:OFF_GATE + D_MODEL])
        gate_sb = jax.nn.sigmoid(proj[..., OFF_GATE + D_MODEL:])

        y_ssm = ssd_mixer(z, xbc, dt_raw, conv_w[i], conv_b[i], dt_bias[i], a_log[i],
                          d_skip[i], ssm_norm_w[i]) @ w_ssm_out[i]
        y_sb = stick_breaking_attention(q, k, v) @ w_sb_out[i]
        mix = (gate_ssm * y_ssm + gate_sb * y_sb) @ w_out[i]
        h = h + rms_norm(mix, mix_post_w[i])

        f = swiglu(rms_norm(h, ffn2_pre_w[i]), ffn2_w1[i], ffn2_w3[i], ffn2_w2[i])
        h = h + 0.5 * rms_norm(f, ffn2_post_w[i])

        ple = (p[i].astype(h.dtype) @ ple_w_proj[i]) * jax.nn.sigmoid(h @ ple_w_gate[i])
        h = h + rms_norm(ple, ple_norm_w[i])
    return h
```

```python
import functools
import math

import jax
import jax.numpy as jnp
import numpy as np
from jax.experimental import pallas as pl
from jax.experimental.pallas import tpu as pltpu

F32 = jnp.float32
BF16 = jnp.bfloat16

NORM_EPS = 1e-6
SSM_HEAD_DIM = 64
SSM_GROUPS = 8
SSM_D_STATE = 128
SSM_CONV = 4
SSM_CHUNK = 128
SB_HEAD_DIM = 128
SB_BLOCK = 128
LANES = 128
CONV_PAD_ROWS = 8
VMEM_LIMIT = 56 * 1024 * 1024
SB_ZERO_LOG = -105.0


def _params(semantics):
    return pltpu.CompilerParams(dimension_semantics=semantics, vmem_limit_bytes=VMEM_LIMIT)


def _split_bf16(x, parts):
    out = []
    rem = x
    for _ in range(parts):
        piece = rem.astype(BF16)
        out.append(piece)
        rem = rem - piece.astype(F32)
    return out


def _dot(a, b):
    return jnp.dot(a, b, preferred_element_type=F32)


def _dot_01(x, m01, parts, m01_on_left=False):
    acc = None
    for piece in _split_bf16(x, parts):
        term = _dot(m01, piece) if m01_on_left else _dot(piece, m01)
        acc = term if acc is None else acc + term
    return acc


def _softplus(x):
    return jnp.maximum(x, 0.0) + jnp.log1p(jnp.exp(-jnp.abs(x)))


def _silu(x):
    return x * jax.nn.sigmoid(x)


def _rms(x, w):
    return x * jax.lax.rsqrt(jnp.mean(x * x, axis=-1, keepdims=True) + NORM_EPS) * w


def _rowwise_kernel(*refs, fn, n_rows, n_vecs):
    rows = [r[...] for r in refs[:n_rows]]
    vecs = [r[...] for r in refs[n_rows:n_rows + n_vecs]]
    outs = fn(rows, vecs)
    for o_ref, o in zip(refs[n_rows + n_vecs:], outs, strict=True):
        o_ref[...] = o.astype(o_ref.dtype)


def _rowwise(fn, rows, vecs, out_dtypes, name, tm=256):
    t, d = rows[0].shape
    tm = min(tm, t)
    row_spec = pl.BlockSpec((tm, d), lambda i: (i, 0))
    vec_spec = pl.BlockSpec((1, d), lambda i: (0, 0))
    return pl.pallas_call(
        functools.partial(_rowwise_kernel, fn=fn, n_rows=len(rows), n_vecs=len(vecs)),
        grid=(t // tm,),
        in_specs=[row_spec] * len(rows) + [vec_spec] * len(vecs),
        out_specs=[row_spec] * len(out_dtypes),
        out_shape=[jax.ShapeDtypeStruct((t, d), dt) for dt in out_dtypes],
        compiler_params=_params(("parallel",)),
        name=name,
    )(*rows, *[v.reshape(1, d) for v in vecs])


def _norm_fn(rows, vecs):
    return (_rms(rows[0], vecs[0]),)


def _residual_norm_fn(rows, vecs, *, scale):
    h = rows[0] + scale * _rms(rows[1], vecs[0])
    return h, _rms(h, vecs[1])


def _residual_cast_fn(rows, vecs, *, scale):
    h = rows[0] + scale * _rms(rows[1], vecs[0])
    return h, h


def _residual_fn(rows, vecs, *, scale):
    return (rows[0] + scale * _rms(rows[1], vecs[0]),)


def _ffn_kernel(xn_ref, w1_ref, w3_ref, w2_ref, o_ref):
    @pl.when(pl.program_id(1) == 0)
    def _():
        o_ref[...] = jnp.zeros_like(o_ref)

    x = xn_ref[...]
    h1 = _dot(x, w1_ref[...])
    h3 = _dot(x, w3_ref[...])
    g = (_silu(h1) * h3).astype(BF16)
    o_ref[...] += _dot(g, w2_ref[...])


def _ffn(xn, w1, w3, w2, name, tm=512, tf=256):
    t, d = xn.shape
    f = w1.shape[1]
    tm = min(tm, t)
    return pl.pallas_call(
        _ffn_kernel,
        grid=(t // tm, f // tf),
        in_specs=[
            pl.BlockSpec((tm, d), lambda i, j: (i, 0)),
            pl.BlockSpec((d, tf), lambda i, j: (0, j)),
            pl.BlockSpec((d, tf), lambda i, j: (0, j)),
            pl.BlockSpec((tf, d), lambda i, j: (j, 0)),
        ],
        out_specs=pl.BlockSpec((tm, d), lambda i, j: (i, 0)),
        out_shape=jax.ShapeDtypeStruct((t, d), F32),
        compiler_params=_params(("parallel", "arbitrary")),
        name=name,
    )(xn, w1, w3, w2)


def _mm_kernel(*refs, n_pairs, epilogue):
    a_refs = refs[:n_pairs]
    w_refs = refs[n_pairs:2 * n_pairs]
    e_refs = refs[2 * n_pairs:-1]
    o_ref = refs[-1]
    accs = [_dot(a[...], w[...]) for a, w in zip(a_refs, w_refs, strict=True)]
    o_ref[...] = epilogue(accs, [e[...] for e in e_refs]).astype(o_ref.dtype)


def _mm(pairs, out_dtype, name, epilogue, extras=(), tm=1024, tn=1024):
    t = pairs[0][0].shape[0]
    n = pairs[0][1].shape[1]
    tn = min(tn, n)
    tm = min(tm, t)
    a_specs = [pl.BlockSpec((tm, a.shape[1]), lambda i, j: (i, 0)) for a, _ in pairs]
    w_specs = [pl.BlockSpec((w.shape[0], tn), lambda i, j: (0, j)) for _, w in pairs]
    e_specs = [pl.BlockSpec((tm, tn), functools.partial(lambda i, j, off: (i, j + off), off=off))
               for _, off in extras]
    return pl.pallas_call(
        functools.partial(_mm_kernel, n_pairs=len(pairs), epilogue=epilogue),
        grid=(t // tm, n // tn),
        in_specs=a_specs + w_specs + e_specs,
        out_specs=pl.BlockSpec((tm, tn), lambda i, j: (i, j)),
        out_shape=jax.ShapeDtypeStruct((t, n), out_dtype),
        compiler_params=_params(("parallel", "parallel")),
        name=name,
    )(*[a for a, _ in pairs], *[w for _, w in pairs], *[e for e, _ in extras])


def _ep_identity(accs, extras):
    return accs[0]


def _ep_sigmoid(accs, extras):
    return jax.nn.sigmoid(accs[0])


def _ep_gated_sum(accs, extras):
    return extras[0].astype(F32) * accs[0] + extras[1].astype(F32) * accs[1]


def _ep_ple(accs, extras):
    return accs[0] * jax.nn.sigmoid(accs[1])


def _ssd_kernel(zx_ref, dt_ref, convw_ref, convb_ref, dtb_ref, alog_ref, dskip_ref, normw_ref,
                tri_ref, expand_ref, o_ref, ext_ref, state_ref, y_ref, *, d_inner):
    l = SSM_CHUNK
    n = SSM_D_STATE
    gw = d_inner // SSM_GROUPS
    heads_per_group = gw // SSM_HEAD_DIM
    off_b = d_inner
    off_c = d_inner + SSM_GROUPS * n

    @pl.when(pl.program_id(1) == 0)
    def _():
        state_ref[...] = jnp.zeros_like(state_ref)
        ext_ref[0:CONV_PAD_ROWS, :] = jnp.zeros((CONV_PAD_ROWS, ext_ref.shape[1]), F32)

    ext_ref[CONV_PAD_ROWS:CONV_PAD_ROWS + l, :] = zx_ref[:, d_inner:].astype(F32)

    def conv_silu(start, width):
        cols = slice(start, start + width)
        acc = convb_ref[:, cols]
        for k in range(SSM_CONV):
            row0 = CONV_PAD_ROWS - (SSM_CONV - 1) + k
            acc = acc + convw_ref[k:k + 1, cols] * ext_ref[row0:row0 + l, cols]
        return _silu(acc)

    dt = _softplus(dt_ref[...] + dtb_ref[...])
    da = dt * (-jnp.exp(alog_ref[...]))
    a_cum = _dot_01(da, tri_ref[...], 3, m01_on_left=True)
    a_cum_t = a_cum.T
    out_decay = jnp.exp(a_cum)
    state_decay = jnp.exp(a_cum[l - 1:l, :] - a_cum)
    expand = expand_ref[...]
    dt_full = _dot_01(dt, expand, 2)
    out_decay_full = _dot_01(out_decay, expand, 2)
    state_decay_full = _dot_01(state_decay, expand, 2)

    row = jax.lax.broadcasted_iota(jnp.int32, (l, l), 0)
    col = jax.lax.broadcasted_iota(jnp.int32, (l, l), 1)
    causal = col <= row
    lane = jax.lax.broadcasted_iota(jnp.int32, (l, 2 * SSM_HEAD_DIM), 1)
    first_head = lane < SSM_HEAD_DIM

    for g in range(SSM_GROUPS):
        ch = slice(g * gw, (g + 1) * gw)
        xs = conv_silu(g * gw, gw)
        bm = conv_silu(off_b + g * n, n)
        cm = conv_silu(off_c + g * n, n).astype(BF16)
        bm_t = bm.T.astype(BF16)
        xdt = xs * dt_full[:, ch]
        cb = _dot(cm, bm_t)

        y_parts = []
        for pair in range(heads_per_group // 2):
            gs = []
            for r in (2 * pair, 2 * pair + 1):
                h = g * heads_per_group + r
                diff = a_cum[:, h:h + 1] - a_cum_t[h:h + 1, :]
                decay = jnp.exp(jnp.where(causal, diff, -1e30))
                gs.append((cb * decay).astype(BF16))
            x_pair = xdt[:, pair * 2 * SSM_HEAD_DIM:(pair + 1) * 2 * SSM_HEAD_DIM]
            block_diag = jnp.concatenate(
                [jnp.where(first_head, x_pair, 0.0).astype(BF16),
                 jnp.where(first_head, 0.0, x_pair).astype(BF16)], axis=0)
            y_parts.append(_dot(jnp.concatenate(gs, axis=1), block_diag))
        y_diag = jnp.concatenate(y_parts, axis=1)

        prev = state_ref[:, ch]
        y_off = _dot(cm, prev.astype(BF16)) * out_decay_full[:, ch]
        new_states = _dot(bm_t, (xdt * state_decay_full[:, ch]).astype(BF16))
        state_ref[:, ch] = prev * out_decay_full[l - 1:l, ch] + new_states

        y = y_diag + y_off + dskip_ref[:, ch] * xs
        y_ref[:, ch] = y * _silu(zx_ref[:, ch].astype(F32))

    ext_ref[0:CONV_PAD_ROWS, :] = ext_ref[l:l + CONV_PAD_ROWS, :]
    o_ref[...] = _rms(y_ref[...], normw_ref[...]).astype(o_ref.dtype)


def _ssd(zx, dt_raw, conv_w, conv_b, dt_bias, a_log, d_skip, norm_w, bsz, d_inner):
    t, width = zx.shape
    conv_dim = width - d_inner
    n_heads = d_inner // SSM_HEAD_DIM
    nc = t // bsz // SSM_CHUNK
    l = SSM_CHUNK

    def pad_heads(v):
        return jnp.zeros((1, LANES), F32).at[0, :n_heads].set(v.astype(F32))

    tri = jnp.asarray(np.tril(np.ones((l, l), np.float32)), BF16)
    expand = np.zeros((LANES, d_inner), np.float32)
    expand[np.arange(d_inner) // SSM_HEAD_DIM, np.arange(d_inner)] = 1.0
    expand = jnp.asarray(expand, BF16)
    d_skip_full = jnp.repeat(d_skip.astype(F32), SSM_HEAD_DIM).reshape(1, d_inner)

    def whole(shape):
        return pl.BlockSpec(shape, lambda b, c: (0, 0))

    return pl.pallas_call(
        functools.partial(_ssd_kernel, d_inner=d_inner),
        grid=(bsz, nc),
        in_specs=[
            pl.BlockSpec((l, width), lambda b, c: (b * nc + c, 0)),
            pl.BlockSpec((l, LANES), lambda b, c: (b * nc + c, 0)),
            whole((SSM_CONV, conv_dim)), whole((1, conv_dim)), whole((1, LANES)), whole((1, LANES)),
            whole((1, d_inner)), whole((1, d_inner)), whole((l, l)), whole((LANES, d_inner)),
        ],
        out_specs=pl.BlockSpec((l, d_inner), lambda b, c: (b * nc + c, 0)),
        out_shape=jax.ShapeDtypeStruct((t, d_inner), BF16),
        scratch_shapes=[
            pltpu.VMEM((CONV_PAD_ROWS + l, conv_dim), F32),
            pltpu.VMEM((SSM_D_STATE, d_inner), F32),
            pltpu.VMEM((l, d_inner), F32),
        ],
        compiler_params=_params(("arbitrary", "arbitrary")),
        name="ssd_mixer",
    )(zx, dt_raw, conv_w.astype(F32), conv_b.astype(F32).reshape(1, conv_dim), pad_heads(dt_bias),
      pad_heads(a_log), d_skip_full, norm_w.astype(F32).reshape(1, d_inner), tri, expand)


def _sb_kernel(q_ref, k_ref, v_ref, suffix_ref, o_ref, *, scale):
    blk = SB_BLOCK
    nq = q_ref.shape[0] // blk
    suffix01 = suffix_ref[...]
    row = jax.lax.broadcasted_iota(jnp.int32, (blk, blk), 0)
    col = jax.lax.broadcasted_iota(jnp.int32, (blk, blk), 1)
    strictly_causal = col < row

    def visit(q, j, carry, acc, mask):
        ks = pl.ds(pl.multiple_of(j * blk, blk), blk)
        z = jax.lax.dot_general(q, k_ref[ks, :], (((1,), (1,)), ((), ())),
                                preferred_element_type=F32) * scale
        sp = _softplus(z)
        log_beta = z - sp
        log_1mb = -sp
        if mask is not None:
            log_1mb = jnp.where(mask, log_1mb, 0.0)
        hi, lo = _split_bf16(log_1mb, 2)
        sums = _dot(jnp.concatenate([hi, lo], axis=1), suffix01)
        w = jnp.exp(log_beta + sums[:, :blk] + carry)
        if mask is not None:
            w = jnp.where(mask, w, 0.0)
        acc = acc + _dot(w.astype(BF16), v_ref[ks, :])
        return carry + sums[:, blk:], acc

    def q_block(i, _):
        qs = pl.ds(pl.multiple_of(i * blk, blk), blk)
        q = q_ref[qs, :]
        zeros = jnp.zeros((blk, blk), F32)
        carry, acc = visit(q, i, zeros, zeros, strictly_causal)

        def cond(state):
            j, carry, _ = state
            return jnp.logical_and(j >= 0, jnp.max(carry) > SB_ZERO_LOG)

        def body(state):
            j, carry, acc = state
            carry, acc = visit(q, j, carry, acc, None)
            return j - 1, carry, acc

        _, _, acc = jax.lax.while_loop(cond, body, (i - 1, carry, acc))
        o_ref[qs, :] = acc.astype(o_ref.dtype)
        return 0

    jax.lax.fori_loop(0, nq, q_block, 0)


def _stick_breaking(qkv, bsz, n_heads):
    t = qkv.shape[0]
    s = t // bsz
    dh = SB_HEAD_DIM
    blk = SB_BLOCK
    j_idx = np.arange(blk)[:, None]
    s_idx = np.arange(blk)[None, :]
    half = np.concatenate([(j_idx > s_idx).astype(np.float32), np.ones((blk, blk), np.float32)], axis=1)
    suffix01 = jnp.asarray(np.concatenate([half, half], axis=0), BF16)

    def head_spec(which):
        return pl.BlockSpec((s, dh), functools.partial(lambda b, h, which: (b, which * n_heads + h), which=which))

    return pl.pallas_call(
        functools.partial(_sb_kernel, scale=float(1.0 / np.sqrt(dh).astype(np.float32))),
        grid=(bsz, n_heads),
        in_specs=[head_spec(0), head_spec(1), head_spec(2),
                  pl.BlockSpec((2 * blk, 2 * blk), lambda b, h: (0, 0))],
        out_specs=pl.BlockSpec((s, dh), lambda b, h: (b, h)),
        out_shape=jax.ShapeDtypeStruct((t, n_heads * dh), BF16),
        compiler_params=_params(("parallel", "parallel")),
        name="stick_breaking",
    )(qkv, qkv, qkv, suffix01)


def _layer(h, p, ffn1_pre_w, ffn1_w1, ffn1_w3, ffn1_w2, ffn1_post_w, mix_pre_w, w_in, conv_w, conv_b,
           dt_bias, a_log, d_skip, ssm_norm_w, w_ssm_out, w_sb_out, w_out, mix_post_w, ffn2_pre_w,
           ffn2_w1, ffn2_w3, ffn2_w2, ffn2_post_w, ple_w_gate, ple_w_proj, ple_norm_w, bsz):
    t, d = h.shape
    d_inner = w_ssm_out.shape[0]
    sb_width = w_sb_out.shape[0]
    n_ssm_heads = d_inner // SSM_HEAD_DIM
    conv_dim = conv_w.shape[1]
    bf = lambda w: w.astype(BF16)

    (xn,) = _rowwise(_norm_fn, [h], [ffn1_pre_w], [BF16], "ffn1_pre_norm")
    f = _ffn(xn, bf(ffn1_w1), bf(ffn1_w3), bf(ffn1_w2), "ffn1")
    h, u = _rowwise(functools.partial(_residual_norm_fn, scale=0.5), [h, f], [ffn1_post_w, mix_pre_w],
                    [F32, BF16], "ffn1_residual_mix_norm")

    off_xbc = d_inner
    off_dt = off_xbc + conv_dim
    off_q = off_dt + n_ssm_heads
    off_gate = off_q + 3 * sb_width
    zx = _mm([(u, bf(w_in[:, :off_dt]))], BF16, "proj_z_xbc", _ep_identity)
    w_dt = jnp.zeros((d, LANES), BF16).at[:, :n_ssm_heads].set(bf(w_in[:, off_dt:off_q]))
    dt_raw = _mm([(u, w_dt)], F32, "proj_dt", _ep_identity)
    qkv = _mm([(u, bf(w_in[:, off_q:off_gate]))], BF16, "proj_qkv", _ep_identity)
    gates = _mm([(u, bf(w_in[:, off_gate:]))], BF16, "proj_gates", _ep_sigmoid)

    y_ssm = _ssd(zx, dt_raw, conv_w, conv_b, dt_bias, a_log, d_skip, ssm_norm_w, bsz, d_inner)
    y_sb = _stick_breaking(qkv, bsz, sb_width // SB_HEAD_DIM)

    tn = 512
    merged = _mm([(y_ssm, bf(w_ssm_out)), (y_sb, bf(w_sb_out))], BF16, "branch_out_gated", _ep_gated_sum,
                 extras=[(gates, 0), (gates, d // tn)], tm=512, tn=tn)
    mix = _mm([(merged, bf(w_out))], F32, "mix_out", _ep_identity)
    h, xn = _rowwise(functools.partial(_residual_norm_fn, scale=1.0), [h, mix], [mix_post_w, ffn2_pre_w],
                     [F32, BF16], "mix_residual_ffn2_norm")

    f = _ffn(xn, bf(ffn2_w1), bf(ffn2_w3), bf(ffn2_w2), "ffn2")
    h, h_bf = _rowwise(functools.partial(_residual_cast_fn, scale=0.5), [h, f], [ffn2_post_w],
                       [F32, BF16], "ffn2_residual")

    ple = _mm([(bf(p), bf(ple_w_proj)), (h_bf, bf(ple_w_gate))], F32, "ple_gated", _ep_ple)
    (h,) = _rowwise(functools.partial(_residual_fn, scale=1.0), [h, ple], [ple_norm_w], [F32], "ple_residual")
    return h


def kernel(x, p, ffn1_pre_w, ffn1_w1, ffn1_w3, ffn1_w2, ffn1_post_w, mix_pre_w, w_in, conv_w, conv_b, dt_bias, a_log, d_skip, ssm_norm_w, w_ssm_out, w_sb_out, w_out, mix_post_w, ffn2_pre_w, ffn2_w1, ffn2_w3, ffn2_w2, ffn2_post_w, ple_w_gate, ple_w_proj, ple_norm_w):
    bsz, seqlen, d = x.shape
    depth = p.shape[0]
    h = x.reshape(bsz * seqlen, d)
    per_layer = (ffn1_pre_w, ffn1_w1, ffn1_w3, ffn1_w2, ffn1_post_w, mix_pre_w, w_in, conv_w, conv_b,
                 dt_bias, a_log, d_skip, ssm_norm_w, w_ssm_out, w_sb_out, w_out, mix_post_w, ffn2_pre_w,
                 ffn2_w1, ffn2_w3, ffn2_w2, ffn2_post_w, ple_w_gate, ple_w_proj, ple_norm_w)
    for i in range(depth):
        h = _layer(h, p[i].reshape(bsz * seqlen, -1), *[w[i] for w in per_layer], bsz=bsz)
    return h.reshape(bsz, seqlen, d)
```

```python
import functools
import math

import jax
import jax.numpy as jnp
import numpy as np
from jax.experimental import pallas as pl
from jax.experimental.pallas import tpu as pltpu

F32 = jnp.float32
BF16 = jnp.bfloat16

NORM_EPS = 1e-6
SSM_HEAD_DIM = 64
SSM_GROUPS = 8
SSM_D_STATE = 128
SSM_CONV = 4
SSM_CHUNK = 128
SB_HEAD_DIM = 128
SB_BLOCK = 128
SB_WINDOW = 3
SB_GROUP = 4
LANES = 128
CONV_PAD_ROWS = 8
VMEM_LIMIT = 56 * 1024 * 1024
SB_ZERO_LOG = -105.0


def _params(semantics):
    return pltpu.CompilerParams(dimension_semantics=semantics, vmem_limit_bytes=VMEM_LIMIT)


def _split_bf16(x, parts):
    out = []
    rem = x
    for _ in range(parts):
        piece = rem.astype(BF16)
        out.append(piece)
        rem = rem - piece.astype(F32)
    return out


def _dot(a, b):
    return jnp.dot(a, b, preferred_element_type=F32)


def _dot_01(x, m01, parts, m01_on_left=False):
    acc = None
    for piece in _split_bf16(x, parts):
        term = _dot(m01, piece) if m01_on_left else _dot(piece, m01)
        acc = term if acc is None else acc + term
    return acc


def _softplus(x):
    return jnp.maximum(x, 0.0) + jnp.log1p(jnp.exp(-jnp.abs(x)))


def _silu(x):
    return x * jax.nn.sigmoid(x)


def _rms(x, w):
    x = x.astype(F32)
    return x * jax.lax.rsqrt(jnp.mean(x * x, axis=-1, keepdims=True) + NORM_EPS) * w


def _rowwise_kernel(*refs, fn, n_rows, n_vecs):
    rows = [r[...] for r in refs[:n_rows]]
    vecs = [r[...] for r in refs[n_rows:n_rows + n_vecs]]
    outs = fn(rows, vecs)
    for o_ref, o in zip(refs[n_rows + n_vecs:], outs, strict=True):
        o_ref[...] = o.astype(o_ref.dtype)


def _rowwise(fn, rows, vecs, out_dtypes, name, tm=256):
    t, d = rows[0].shape
    tm = min(tm, t)
    row_spec = pl.BlockSpec((tm, d), lambda i: (i, 0))
    vec_spec = pl.BlockSpec((1, d), lambda i: (0, 0))
    return pl.pallas_call(
        functools.partial(_rowwise_kernel, fn=fn, n_rows=len(rows), n_vecs=len(vecs)),
        grid=(t // tm,),
        in_specs=[row_spec] * len(rows) + [vec_spec] * len(vecs),
        out_specs=[row_spec] * len(out_dtypes),
        out_shape=[jax.ShapeDtypeStruct((t, d), dt) for dt in out_dtypes],
        compiler_params=_params(("parallel",)),
        name=name,
    )(*rows, *[v.reshape(1, d) for v in vecs])


def _norm_fn(rows, vecs):
    return (_rms(rows[0], vecs[0]),)


def _residual_norm_fn(rows, vecs, *, scale):
    h = rows[0] + scale * _rms(rows[1], vecs[0])
    return h, _rms(h, vecs[1])


def _residual_cast_fn(rows, vecs, *, scale):
    h = rows[0] + scale * _rms(rows[1], vecs[0])
    return h, h


def _residual_fn(rows, vecs, *, scale):
    return (rows[0] + scale * _rms(rows[1], vecs[0]),)


def _ffn_kernel(xn_ref, w1_ref, w3_ref, w2_ref, o_ref, acc_ref):
    @pl.when(pl.program_id(1) == 0)
    def _():
        acc_ref[...] = jnp.zeros_like(acc_ref)

    x = xn_ref[...]
    h1 = _dot(x, w1_ref[...])
    h3 = _dot(x, w3_ref[...])
    g = (_silu(h1) * h3).astype(BF16)
    acc_ref[...] += _dot(g, w2_ref[...])

    @pl.when(pl.program_id(1) == pl.num_programs(1) - 1)
    def _():
        o_ref[...] = acc_ref[...].astype(o_ref.dtype)


def _ffn(xn, w1, w3, w2, name, tm=512, tf=256):
    t, d = xn.shape
    f = w1.shape[1]
    tm = min(tm, t)
    return pl.pallas_call(
        _ffn_kernel,
        grid=(t // tm, f // tf),
        in_specs=[
            pl.BlockSpec((tm, d), lambda i, j: (i, 0)),
            pl.BlockSpec((d, tf), lambda i, j: (0, j)),
            pl.BlockSpec((d, tf), lambda i, j: (0, j)),
            pl.BlockSpec((tf, d), lambda i, j: (j, 0)),
        ],
        out_specs=pl.BlockSpec((tm, d), lambda i, j: (i, 0)),
        out_shape=jax.ShapeDtypeStruct((t, d), BF16),
        scratch_shapes=[pltpu.VMEM((tm, d), F32)],
        compiler_params=_params(("parallel", "arbitrary")),
        name=name,
    )(xn, w1, w3, w2)


def _mm_kernel(*refs, n_pairs, epilogue):
    a_refs = refs[:n_pairs]
    w_refs = refs[n_pairs:2 * n_pairs]
    e_refs = refs[2 * n_pairs:-1]
    o_ref = refs[-1]
    accs = [_dot(a[...], w[...]) for a, w in zip(a_refs, w_refs, strict=True)]
    o_ref[...] = epilogue(accs, [e[...] for e in e_refs]).astype(o_ref.dtype)


def _mm(pairs, out_dtype, name, epilogue, extras=(), tm=1024, tn=1024):
    t = pairs[0][0].shape[0]
    n = pairs[0][1].shape[1]
    tn = min(tn, n)
    tm = min(tm, t)
    a_specs = [pl.BlockSpec((tm, a.shape[1]), lambda i, j: (i, 0)) for a, _ in pairs]
    w_specs = [pl.BlockSpec((w.shape[0], tn), lambda i, j: (0, j)) for _, w in pairs]
    e_specs = [pl.BlockSpec((tm, tn), functools.partial(lambda i, j, off: (i, j + off), off=off))
               for _, off in extras]
    return pl.pallas_call(
        functools.partial(_mm_kernel, n_pairs=len(pairs), epilogue=epilogue),
        grid=(t // tm, n // tn),
        in_specs=a_specs + w_specs + e_specs,
        out_specs=pl.BlockSpec((tm, tn), lambda i, j: (i, j)),
        out_shape=jax.ShapeDtypeStruct((t, n), out_dtype),
        compiler_params=_params(("parallel", "parallel")),
        name=name,
    )(*[a for a, _ in pairs], *[w for _, w in pairs], *[e for e, _ in extras])


def _ep_identity(accs, extras):
    return accs[0]


def _ep_sigmoid(accs, extras):
    return jax.nn.sigmoid(accs[0])


def _ep_gated_sum(accs, extras):
    return extras[0].astype(F32) * accs[0] + extras[1].astype(F32) * accs[1]


def _ep_ple(accs, extras):
    return accs[0] * jax.nn.sigmoid(accs[1])


def _ssd_kernel(zx_ref, dt_ref, convw_ref, convb_ref, dtb_ref, alog_ref, dskip_ref, normw_ref,
                tri_ref, expand_ref, o_ref, ext_ref, state_ref, y_ref, *, d_inner):
    l = SSM_CHUNK
    n = SSM_D_STATE
    gw = d_inner // SSM_GROUPS
    heads_per_group = gw // SSM_HEAD_DIM
    off_b = d_inner
    off_c = d_inner + SSM_GROUPS * n

    @pl.when(pl.program_id(1) == 0)
    def _():
        state_ref[...] = jnp.zeros_like(state_ref)
        ext_ref[0:CONV_PAD_ROWS, :] = jnp.zeros((CONV_PAD_ROWS, ext_ref.shape[1]), F32)

    ext_ref[CONV_PAD_ROWS:CONV_PAD_ROWS + l, :] = zx_ref[:, d_inner:].astype(F32)

    def conv_silu(start, width):
        cols = slice(start, start + width)
        acc = convb_ref[:, cols]
        for k in range(SSM_CONV):
            row0 = CONV_PAD_ROWS - (SSM_CONV - 1) + k
            acc = acc + convw_ref[k:k + 1, cols] * ext_ref[row0:row0 + l, cols]
        return _silu(acc)

    dt = _softplus(dt_ref[...] + dtb_ref[...])
    da = dt * (-jnp.exp(alog_ref[...]))
    a_cum = _dot_01(da, tri_ref[...], 3, m01_on_left=True)
    a_cum_t = a_cum.T
    out_decay = jnp.exp(a_cum)
    state_decay = jnp.exp(a_cum[l - 1:l, :] - a_cum)
    expand = expand_ref[...]
    dt_full = _dot_01(dt, expand, 2)
    out_decay_full = _dot_01(out_decay, expand, 2)
    state_decay_full = _dot_01(state_decay, expand, 2)

    row = jax.lax.broadcasted_iota(jnp.int32, (l, l), 0)
    col = jax.lax.broadcasted_iota(jnp.int32, (l, l), 1)
    causal = col <= row
    lane = jax.lax.broadcasted_iota(jnp.int32, (l, 2 * SSM_HEAD_DIM), 1)
    first_head = lane < SSM_HEAD_DIM

    for g in range(SSM_GROUPS):
        ch = slice(g * gw, (g + 1) * gw)
        xs = conv_silu(g * gw, gw)
        bm = conv_silu(off_b + g * n, n)
        cm = conv_silu(off_c + g * n, n).astype(BF16)
        bm_t = bm.T.astype(BF16)
        xdt = xs * dt_full[:, ch]
        cb = _dot(cm, bm_t)

        y_parts = []
        for pair in range(heads_per_group // 2):
            gs = []
            for r in (2 * pair, 2 * pair + 1):
                h = g * heads_per_group + r
                diff = a_cum[:, h:h + 1] - a_cum_t[h:h + 1, :]
                decay = jnp.exp(jnp.where(causal, diff, -1e30))
                gs.append((cb * decay).astype(BF16))
            x_pair = xdt[:, pair * 2 * SSM_HEAD_DIM:(pair + 1) * 2 * SSM_HEAD_DIM]
            block_diag = jnp.concatenate(
                [jnp.where(first_head, x_pair, 0.0).astype(BF16),
                 jnp.where(first_head, 0.0, x_pair).astype(BF16)], axis=0)
            y_parts.append(_dot(jnp.concatenate(gs, axis=1), block_diag))
        y_diag = jnp.concatenate(y_parts, axis=1)

        prev = state_ref[:, ch]
        y_off = _dot(cm, prev.astype(BF16)) * out_decay_full[:, ch]
        new_states = _dot(bm_t, (xdt * state_decay_full[:, ch]).astype(BF16))
        state_ref[:, ch] = prev * out_decay_full[l - 1:l, ch] + new_states

        y = y_diag + y_off + dskip_ref[:, ch] * xs
        y_ref[:, ch] = y * _silu(zx_ref[:, ch].astype(F32))

    ext_ref[0:CONV_PAD_ROWS, :] = ext_ref[l:l + CONV_PAD_ROWS, :]
    o_ref[...] = _rms(y_ref[...], normw_ref[...]).astype(o_ref.dtype)


def _ssd(zx, dt_raw, conv_w, conv_b, dt_bias, a_log, d_skip, norm_w, bsz, d_inner):
    t, width = zx.shape
    conv_dim = width - d_inner
    n_heads = d_inner // SSM_HEAD_DIM
    nc = t // bsz // SSM_CHUNK
    l = SSM_CHUNK

    def pad_heads(v):
        return jnp.zeros((1, LANES), F32).at[0, :n_heads].set(v.astype(F32))

    tri = jnp.asarray(np.tril(np.ones((l, l), np.float32)), BF16)
    expand = np.zeros((LANES, d_inner), np.float32)
    expand[np.arange(d_inner) // SSM_HEAD_DIM, np.arange(d_inner)] = 1.0
    expand = jnp.asarray(expand, BF16)
    d_skip_full = jnp.repeat(d_skip.astype(F32), SSM_HEAD_DIM).reshape(1, d_inner)

    def whole(shape):
        return pl.BlockSpec(shape, lambda b, c: (0, 0))

    return pl.pallas_call(
        functools.partial(_ssd_kernel, d_inner=d_inner),
        grid=(bsz, nc),
        in_specs=[
            pl.BlockSpec((l, width), lambda b, c: (b * nc + c, 0)),
            pl.BlockSpec((l, LANES), lambda b, c: (b * nc + c, 0)),
            whole((SSM_CONV, conv_dim)), whole((1, conv_dim)), whole((1, LANES)), whole((1, LANES)),
            whole((1, d_inner)), whole((1, d_inner)), whole((l, l)), whole((LANES, d_inner)),
        ],
        out_specs=pl.BlockSpec((l, d_inner), lambda b, c: (b * nc + c, 0)),
        out_shape=jax.ShapeDtypeStruct((t, d_inner), BF16),
        scratch_shapes=[
            pltpu.VMEM((CONV_PAD_ROWS + l, conv_dim), F32),
            pltpu.VMEM((SSM_D_STATE, d_inner), F32),
            pltpu.VMEM((l, d_inner), F32),
        ],
        compiler_params=_params(("arbitrary", "arbitrary")),
        name="ssd_mixer",
    )(zx, dt_raw, conv_w.astype(F32), conv_b.astype(F32).reshape(1, conv_dim), pad_heads(dt_bias),
      pad_heads(a_log), d_skip_full, norm_w.astype(F32).reshape(1, d_inner), tri, expand)


def _sb_kernel(q_ref, k_ref, v_ref, suffix_ref, o_ref, *, scale):
    blk = SB_BLOCK
    nq = q_ref.shape[0] // blk
    suffix01 = suffix_ref[...]
    row = jax.lax.broadcasted_iota(jnp.int32, (blk, blk), 0)
    col = jax.lax.broadcasted_iota(jnp.int32, (blk, blk), 1)
    strictly_causal = col < row

    def blocks(first, count=1):
        start = first * blk
        if not isinstance(start, int):
            start = pl.multiple_of(start, blk)
        return pl.ds(start, count * blk)

    def visit(q, j, carry, acc, mask):
        ks = blocks(j)
        z = jax.lax.dot_general(q, k_ref[ks, :], (((1,), (1,)), ((), ())),
                                preferred_element_type=F32) * scale
        sp = _softplus(z)
        log_beta = z - sp
        log_1mb = -sp
        if mask is not None:
            log_1mb = jnp.where(mask, log_1mb, 0.0)
        hi, lo = _split_bf16(log_1mb, 2)
        sums = _dot(jnp.concatenate([hi, lo], axis=1), suffix01)
        w = jnp.exp(log_beta + sums[:, :blk] + carry)
        if mask is not None:
            w = jnp.where(mask, w, 0.0)
        acc = acc + _dot(w.astype(BF16), v_ref[ks, :])
        return carry + sums[:, blk:], acc

    def q_slice(i):
        return blocks(i)

    def finish(i, q, j, carry, acc):
        def cond(state):
            j, carry, _ = state
            return jnp.logical_and(j >= 0, jnp.max(carry) > SB_ZERO_LOG)

        def body(state):
            j, carry, acc = state
            carry, acc = visit(q, j, carry, acc, None)
            return j - 1, carry, acc

        _, _, acc = jax.lax.while_loop(cond, body, (jnp.int32(j), carry, acc))
        o_ref[q_slice(i), :] = acc.astype(o_ref.dtype)

    def single(i):
        q = q_ref[q_slice(i), :]
        zeros = jnp.zeros((blk, blk), F32)
        carry, acc = visit(q, i, zeros, zeros, strictly_causal)
        finish(i, q, i - 1, carry, acc)

    last = (SB_WINDOW - 1) * blk

    def window_scores(q, ks):
        z = jax.lax.dot_general(q, k_ref[ks, :], (((1,), (1,)), ((), ())),
                                preferred_element_type=F32) * scale
        sp = _softplus(z)
        log_1mb = jnp.concatenate([-sp[:, :last], jnp.where(strictly_causal, -sp[:, last:], 0.0)], axis=1)
        hi, lo = _split_bf16(log_1mb, 2)
        stacked = jnp.concatenate(
            [jnp.concatenate([hi[:, b * blk:(b + 1) * blk], lo[:, b * blk:(b + 1) * blk]], axis=1)
             for b in range(SB_WINDOW)], axis=0)
        return z - sp, stacked

    def window_weights(log_beta, sums):
        later = jnp.zeros((blk, blk), F32)
        excl = [None] * SB_WINDOW
        for b in reversed(range(SB_WINDOW)):
            rows = slice(b * blk, (b + 1) * blk)
            excl[b] = sums[rows, :blk] + later
            later = later + sums[rows, blk:]
        w = jnp.exp(log_beta + jnp.concatenate(excl, axis=1))
        w = jnp.concatenate([w[:, :last], jnp.where(strictly_causal, w[:, last:], 0.0)], axis=1)
        return w.astype(BF16), later

    def group(first, size):
        idx = [first + g for g in range(size)]
        qs = [q_ref[q_slice(i), :] for i in idx]
        kss = [blocks(i - (SB_WINDOW - 1), SB_WINDOW) for i in idx]
        scored = [window_scores(q, ks) for q, ks in zip(qs, kss)]
        sums = [_dot(stacked, suffix01) for _, stacked in scored]
        weighted = [window_weights(log_beta, s) for (log_beta, _), s in zip(scored, sums)]
        accs = [_dot(w, v_ref[ks, :]) for (w, _), ks in zip(weighted, kss)]
        for i, acc in zip(idx, accs):
            o_ref[q_slice(i), :] = acc.astype(o_ref.dtype)
        carries = [carry for _, carry in weighted]

        @pl.when(jnp.max(functools.reduce(jnp.maximum, carries)) > SB_ZERO_LOG)
        def _():
            for i, q, carry, acc in zip(idx, qs, carries, accs):
                finish(i, q, i - SB_WINDOW, carry, acc)

    def full_group(ig, _):
        group(n_single + SB_GROUP * ig, SB_GROUP)
        return 0

    n_single = min(SB_WINDOW - 1, nq)
    n_groups, n_left = divmod(nq - n_single, SB_GROUP)
    for i in range(n_single):
        single(i)
    jax.lax.fori_loop(0, n_groups, full_group, 0)
    if n_left:
        group(nq - n_left, n_left)


def _stick_breaking(qkv, bsz, n_heads):
    t = qkv.shape[0]
    s = t // bsz
    dh = SB_HEAD_DIM
    blk = SB_BLOCK
    j_idx = np.arange(blk)[:, None]
    s_idx = np.arange(blk)[None, :]
    half = np.concatenate([(j_idx > s_idx).astype(np.float32), np.ones((blk, blk), np.float32)], axis=1)
    suffix01 = jnp.asarray(np.concatenate([half, half], axis=0), BF16)

    def head_spec(which):
        return pl.BlockSpec((s, dh), functools.partial(lambda b, h, which: (b, which * n_heads + h), which=which))

    return pl.pallas_call(
        functools.partial(_sb_kernel, scale=float(1.0 / np.sqrt(dh).astype(np.float32))),
        grid=(bsz, n_heads),
        in_specs=[head_spec(0), head_spec(1), head_spec(2),
                  pl.BlockSpec((2 * blk, 2 * blk), lambda b, h: (0, 0))],
        out_specs=pl.BlockSpec((s, dh), lambda b, h: (b, h)),
        out_shape=jax.ShapeDtypeStruct((t, n_heads * dh), BF16),
        compiler_params=_params(("parallel", "parallel")),
        name="stick_breaking",
    )(qkv, qkv, qkv, suffix01)


def _layer(h, p, ffn1_pre_w, ffn1_w1, ffn1_w3, ffn1_w2, ffn1_post_w, mix_pre_w, w_in, conv_w, conv_b,
           dt_bias, a_log, d_skip, ssm_norm_w, w_ssm_out, w_sb_out, w_out, mix_post_w, ffn2_pre_w,
           ffn2_w1, ffn2_w3, ffn2_w2, ffn2_post_w, ple_w_gate, ple_w_proj, ple_norm_w, bsz):
    t, d = h.shape
    d_inner = w_ssm_out.shape[0]
    sb_width = w_sb_out.shape[0]
    n_ssm_heads = d_inner // SSM_HEAD_DIM
    conv_dim = conv_w.shape[1]
    bf = lambda w: w.astype(BF16)

    (xn,) = _rowwise(_norm_fn, [h], [ffn1_pre_w], [BF16], "ffn1_pre_norm")
    f = _ffn(xn, bf(ffn1_w1), bf(ffn1_w3), bf(ffn1_w2), "ffn1")
    h, u = _rowwise(functools.partial(_residual_norm_fn, scale=0.5), [h, f], [ffn1_post_w, mix_pre_w],
                    [F32, BF16], "ffn1_residual_mix_norm")

    off_xbc = d_inner
    off_dt = off_xbc + conv_dim
    off_q = off_dt + n_ssm_heads
    off_gate = off_q + 3 * sb_width
    zx = _mm([(u, bf(w_in[:, :off_dt]))], BF16, "proj_z_xbc", _ep_identity)
    w_dt = jnp.zeros((d, LANES), BF16).at[:, :n_ssm_heads].set(bf(w_in[:, off_dt:off_q]))
    dt_raw = _mm([(u, w_dt)], F32, "proj_dt", _ep_identity)
    qkv = _mm([(u, bf(w_in[:, off_q:off_gate]))], BF16, "proj_qkv", _ep_identity)
    gates = _mm([(u, bf(w_in[:, off_gate:]))], BF16, "proj_gates", _ep_sigmoid)

    y_ssm = _ssd(zx, dt_raw, conv_w, conv_b, dt_bias, a_log, d_skip, ssm_norm_w, bsz, d_inner)
    y_sb = _stick_breaking(qkv, bsz, sb_width // SB_HEAD_DIM)

    tn = 512
    merged = _mm([(y_ssm, bf(w_ssm_out)), (y_sb, bf(w_sb_out))], BF16, "branch_out_gated", _ep_gated_sum,
                 extras=[(gates, 0), (gates, d // tn)], tm=512, tn=tn)
    mix = _mm([(merged, bf(w_out))], BF16, "mix_out", _ep_identity)
    h, xn = _rowwise(functools.partial(_residual_norm_fn, scale=1.0), [h, mix], [mix_post_w, ffn2_pre_w],
                     [F32, BF16], "mix_residual_ffn2_norm")

    f = _ffn(xn, bf(ffn2_w1), bf(ffn2_w3), bf(ffn2_w2), "ffn2")
    h, h_bf = _rowwise(functools.partial(_residual_cast_fn, scale=0.5), [h, f], [ffn2_post_w],
                       [F32, BF16], "ffn2_residual")

    ple = _mm([(bf(p), bf(ple_w_proj)), (h_bf, bf(ple_w_gate))], BF16, "ple_gated", _ep_ple)
    (h,) = _rowwise(functools.partial(_residual_fn, scale=1.0), [h, ple], [ple_norm_w], [F32], "ple_residual")
    return h


def kernel(x, p, ffn1_pre_w, ffn1_w1, ffn1_w3, ffn1_w2, ffn1_post_w, mix_pre_w, w_in, conv_w, conv_b, dt_bias, a_log, d_skip, ssm_norm_w, w_ssm_out, w_sb_out, w_out, mix_post_w, ffn2_pre_w, ffn2_w1, ffn2_w3, ffn2_w2, ffn2_post_w, ple_w_gate, ple_w_proj, ple_norm_w):
    bsz, seqlen, d = x.shape
    depth = p.shape[0]
    h = x.reshape(bsz * seqlen, d)
    per_layer = (ffn1_pre_w, ffn1_w1, ffn1_w3, ffn1_w2, ffn1_post_w, mix_pre_w, w_in, conv_w, conv_b,
                 dt_bias, a_log, d_skip, ssm_norm_w, w_ssm_out, w_sb_out, w_out, mix_post_w, ffn2_pre_w,
                 ffn2_w1, ffn2_w3, ffn2_w2, ffn2_post_w, ple_w_gate, ple_w_proj, ple_norm_w)
    for i in range(depth):
        h = _layer(h, p[i].reshape(bsz * seqlen, -1), *[w[i] for w in per_layer], bsz=bsz)
    return h.reshape(bsz, seqlen, d)
```

```python
import functools
import math

import jax
import jax.numpy as jnp
import numpy as np
from jax.experimental import pallas as pl
from jax.experimental.pallas import tpu as pltpu

F32 = jnp.float32
BF16 = jnp.bfloat16

NORM_EPS = 1e-6
SSM_HEAD_DIM = 64
SSM_GROUPS = 8
SSM_D_STATE = 128
SSM_CONV = 4
SSM_CHUNK = 128
SB_HEAD_DIM = 128
SB_BLOCK = 128
SB_WINDOW = 3
SB_GROUP = 4
LANES = 128
CONV_PAD_ROWS = 8
VMEM_LIMIT = 56 * 1024 * 1024
FFN_VMEM_LIMIT = 60 * 1024 * 1024
SB_ZERO_LOG = -105.0


def _params(semantics, vmem_limit=VMEM_LIMIT):
    return pltpu.CompilerParams(dimension_semantics=semantics, vmem_limit_bytes=vmem_limit)


def _split_bf16(x, parts):
    out = []
    rem = x
    for _ in range(parts):
        piece = rem.astype(BF16)
        out.append(piece)
        rem = rem - piece.astype(F32)
    return out


def _dot(a, b):
    return jnp.dot(a, b, preferred_element_type=F32)


def _dot_01(x, m01, parts, m01_on_left=False):
    acc = None
    for piece in _split_bf16(x, parts):
        term = _dot(m01, piece) if m01_on_left else _dot(piece, m01)
        acc = term if acc is None else acc + term
    return acc


def _softplus(x):
    return jnp.maximum(x, 0.0) + jnp.log1p(jnp.exp(-jnp.abs(x)))


def _silu(x):
    return x * jax.nn.sigmoid(x)


def _rms(x, w):
    x = x.astype(F32)
    return x * jax.lax.rsqrt(jnp.mean(x * x, axis=-1, keepdims=True) + NORM_EPS) * w


def _rowwise_kernel(*refs, fn, n_rows, n_vecs):
    rows = [r[...] for r in refs[:n_rows]]
    vecs = [r[...] for r in refs[n_rows:n_rows + n_vecs]]
    outs = fn(rows, vecs)
    for o_ref, o in zip(refs[n_rows + n_vecs:], outs, strict=True):
        o_ref[...] = o.astype(o_ref.dtype)


def _rowwise(fn, rows, vecs, out_dtypes, name, tm=256):
    t, d = rows[0].shape
    tm = min(tm, t)
    row_spec = pl.BlockSpec((tm, d), lambda i: (i, 0))
    vec_spec = pl.BlockSpec((1, d), lambda i: (0, 0))
    return pl.pallas_call(
        functools.partial(_rowwise_kernel, fn=fn, n_rows=len(rows), n_vecs=len(vecs)),
        grid=(t // tm,),
        in_specs=[row_spec] * len(rows) + [vec_spec] * len(vecs),
        out_specs=[row_spec] * len(out_dtypes),
        out_shape=[jax.ShapeDtypeStruct((t, d), dt) for dt in out_dtypes],
        compiler_params=_params(("parallel",)),
        name=name,
    )(*rows, *[v.reshape(1, d) for v in vecs])


def _norm_fn(rows, vecs):
    return (_rms(rows[0], vecs[0]),)


def _residual_norm_fn(rows, vecs, *, scale):
    h = rows[0] + scale * _rms(rows[1], vecs[0])
    return h, _rms(h, vecs[1])


def _residual_cast_fn(rows, vecs, *, scale):
    h = rows[0] + scale * _rms(rows[1], vecs[0])
    return h, h


def _residual_fn(rows, vecs, *, scale):
    return (rows[0] + scale * _rms(rows[1], vecs[0]),)


def _ffn_kernel(xn_ref, w1_ref, w3_ref, w2_ref, o_ref, acc_ref):
    @pl.when(pl.program_id(1) == 0)
    def _():
        acc_ref[...] = jnp.zeros_like(acc_ref)

    x = xn_ref[...]
    h1 = _dot(x, w1_ref[...])
    h3 = _dot(x, w3_ref[...])
    g = (_silu(h1) * h3).astype(BF16)
    acc_ref[...] += _dot(g, w2_ref[...])

    @pl.when(pl.program_id(1) == pl.num_programs(1) - 1)
    def _():
        o_ref[...] = acc_ref[...].astype(o_ref.dtype)


def _ffn(xn, w1, w3, w2, name, tm=1024, tf=256):
    t, d = xn.shape
    f = w1.shape[1]
    tm = min(tm, t)
    return pl.pallas_call(
        _ffn_kernel,
        grid=(t // tm, f // tf),
        in_specs=[
            pl.BlockSpec((tm, d), lambda i, j: (i, 0)),
            pl.BlockSpec((d, tf), lambda i, j: (0, j)),
            pl.BlockSpec((d, tf), lambda i, j: (0, j)),
            pl.BlockSpec((tf, d), lambda i, j: (j, 0)),
        ],
        out_specs=pl.BlockSpec((tm, d), lambda i, j: (i, 0), pipeline_mode=pl.Buffered(1)),
        out_shape=jax.ShapeDtypeStruct((t, d), BF16),
        scratch_shapes=[pltpu.VMEM((tm, d), F32)],
        compiler_params=_params(("parallel", "arbitrary"), FFN_VMEM_LIMIT),
        name=name,
    )(xn, w1, w3, w2)


def _mm_kernel(*refs, n_pairs, epilogue):
    a_refs = refs[:n_pairs]
    w_refs = refs[n_pairs:2 * n_pairs]
    e_refs = refs[2 * n_pairs:-1]
    o_ref = refs[-1]
    accs = [_dot(a[...], w[...]) for a, w in zip(a_refs, w_refs, strict=True)]
    o_ref[...] = epilogue(accs, [e[...] for e in e_refs]).astype(o_ref.dtype)


def _mm(pairs, out_dtype, name, epilogue, extras=(), tm=1024, tn=1024):
    t = pairs[0][0].shape[0]
    n = pairs[0][1].shape[1]
    tn = min(tn, n)
    tm = min(tm, t)
    a_specs = [pl.BlockSpec((tm, a.shape[1]), lambda i, j: (i, 0)) for a, _ in pairs]
    w_specs = [pl.BlockSpec((w.shape[0], tn), lambda i, j: (0, j)) for _, w in pairs]
    e_specs = [pl.BlockSpec((tm, tn), functools.partial(lambda i, j, off: (i, j + off), off=off))
               for _, off in extras]
    return pl.pallas_call(
        functools.partial(_mm_kernel, n_pairs=len(pairs), epilogue=epilogue),
        grid=(t // tm, n // tn),
        in_specs=a_specs + w_specs + e_specs,
        out_specs=pl.BlockSpec((tm, tn), lambda i, j: (i, j)),
        out_shape=jax.ShapeDtypeStruct((t, n), out_dtype),
        compiler_params=_params(("parallel", "parallel")),
        name=name,
    )(*[a for a, _ in pairs], *[w for _, w in pairs], *[e for e, _ in extras])


def _mm_f32w_kernel(a_ref, w_ref, o_ref, wbf_ref, *, epilogue):
    @pl.when(pl.program_id(1) == 0)
    def _():
        wbf_ref[...] = w_ref[...].astype(BF16)

    o_ref[...] = epilogue([_dot(a_ref[...], wbf_ref[...])], []).astype(o_ref.dtype)


def _mm_f32w(a, w, out_dtype, name, epilogue, col0=0, n=None, tm=1024, tn=512):
    t, k = a.shape
    n = w.shape[1] - col0 if n is None else n
    tm = min(tm, t)
    tn = min(tn, n)
    assert col0 % tn == 0 and n % tn == 0
    return pl.pallas_call(
        functools.partial(_mm_f32w_kernel, epilogue=epilogue),
        grid=(n // tn, t // tm),
        in_specs=[pl.BlockSpec((tm, k), lambda j, i: (i, 0)),
                  pl.BlockSpec((k, tn), functools.partial(lambda j, i, off: (0, j + off), off=col0 // tn))],
        out_specs=pl.BlockSpec((tm, tn), lambda j, i: (i, j)),
        out_shape=jax.ShapeDtypeStruct((t, n), out_dtype),
        scratch_shapes=[pltpu.VMEM((k, tn), BF16)],
        compiler_params=_params(("parallel", "arbitrary")),
        name=name,
    )(a, w)


def _ep_identity(accs, extras):
    return accs[0]


def _ep_sigmoid(accs, extras):
    return jax.nn.sigmoid(accs[0])


def _ep_gated_sum(accs, extras):
    return extras[0].astype(F32) * accs[0] + extras[1].astype(F32) * accs[1]


def _ep_ple(accs, extras):
    return accs[0] * jax.nn.sigmoid(accs[1])


def _ssd_kernel(zx_ref, dt_ref, convw_ref, convb_ref, dtb_ref, alog_ref, dskip_ref, normw_ref,
                tri_ref, expand_ref, o_ref, ext_ref, state_ref, y_ref, *, d_inner):
    l = SSM_CHUNK
    n = SSM_D_STATE
    gw = d_inner // SSM_GROUPS
    heads_per_group = gw // SSM_HEAD_DIM
    off_b = d_inner
    off_c = d_inner + SSM_GROUPS * n

    @pl.when(pl.program_id(1) == 0)
    def _():
        state_ref[...] = jnp.zeros_like(state_ref)
        ext_ref[0:CONV_PAD_ROWS, :] = jnp.zeros((CONV_PAD_ROWS, ext_ref.shape[1]), F32)

    ext_ref[CONV_PAD_ROWS:CONV_PAD_ROWS + l, :] = zx_ref[:, d_inner:].astype(F32)

    def conv_silu(start, width):
        cols = slice(start, start + width)
        acc = convb_ref[:, cols]
        for k in range(SSM_CONV):
            row0 = CONV_PAD_ROWS - (SSM_CONV - 1) + k
            acc = acc + convw_ref[k:k + 1, cols] * ext_ref[row0:row0 + l, cols]
        return _silu(acc)

    dt = _softplus(dt_ref[...] + dtb_ref[...])
    da = dt * (-jnp.exp(alog_ref[...]))
    a_cum = _dot_01(da, tri_ref[...], 3, m01_on_left=True)
    a_cum_t = a_cum.T
    out_decay = jnp.exp(a_cum)
    state_decay = jnp.exp(a_cum[l - 1:l, :] - a_cum)
    expand = expand_ref[...]
    dt_full = _dot_01(dt, expand, 2)
    out_decay_full = _dot_01(out_decay, expand, 2)
    state_decay_full = _dot_01(state_decay, expand, 2)

    row = jax.lax.broadcasted_iota(jnp.int32, (l, l), 0)
    col = jax.lax.broadcasted_iota(jnp.int32, (l, l), 1)
    causal = col <= row
    lane = jax.lax.broadcasted_iota(jnp.int32, (l, 2 * SSM_HEAD_DIM), 1)
    first_head = lane < SSM_HEAD_DIM

    for g in range(SSM_GROUPS):
        ch = slice(g * gw, (g + 1) * gw)
        xs = conv_silu(g * gw, gw)
        bm = conv_silu(off_b + g * n, n)
        cm = conv_silu(off_c + g * n, n).astype(BF16)
        bm_t = bm.T.astype(BF16)
        xdt = xs * dt_full[:, ch]
        cb = _dot(cm, bm_t)

        y_parts = []
        for pair in range(heads_per_group // 2):
            gs = []
            for r in (2 * pair, 2 * pair + 1):
                h = g * heads_per_group + r
                diff = a_cum[:, h:h + 1] - a_cum_t[h:h + 1, :]
                decay = jnp.exp(jnp.where(causal, diff, -1e30))
                gs.append((cb * decay).astype(BF16))
            x_pair = xdt[:, pair * 2 * SSM_HEAD_DIM:(pair + 1) * 2 * SSM_HEAD_DIM]
            block_diag = jnp.concatenate(
                [jnp.where(first_head, x_pair, 0.0).astype(BF16),
                 jnp.where(first_head, 0.0, x_pair).astype(BF16)], axis=0)
            y_parts.append(_dot(jnp.concatenate(gs, axis=1), block_diag))
        y_diag = jnp.concatenate(y_parts, axis=1)

        prev = state_ref[:, ch]
        y_off = _dot(cm, prev.astype(BF16)) * out_decay_full[:, ch]
        new_states = _dot(bm_t, (xdt * state_decay_full[:, ch]).astype(BF16))
        state_ref[:, ch] = prev * out_decay_full[l - 1:l, ch] + new_states

        y = y_diag + y_off + dskip_ref[:, ch] * xs
        y_ref[:, ch] = y * _silu(zx_ref[:, ch].astype(F32))

    ext_ref[0:CONV_PAD_ROWS, :] = ext_ref[l:l + CONV_PAD_ROWS, :]
    o_ref[...] = _rms(y_ref[...], normw_ref[...]).astype(o_ref.dtype)


def _ssd(zx, dt_raw, conv_w, conv_b, dt_bias, a_log, d_skip, norm_w, bsz, d_inner):
    t, width = zx.shape
    conv_dim = width - d_inner
    n_heads = d_inner // SSM_HEAD_DIM
    nc = t // bsz // SSM_CHUNK
    l = SSM_CHUNK

    def pad_heads(v):
        return jnp.zeros((1, LANES), F32).at[0, :n_heads].set(v.astype(F32))

    tri = jnp.asarray(np.tril(np.ones((l, l), np.float32)), BF16)
    expand = np.zeros((LANES, d_inner), np.float32)
    expand[np.arange(d_inner) // SSM_HEAD_DIM, np.arange(d_inner)] = 1.0
    expand = jnp.asarray(expand, BF16)
    d_skip_full = jnp.repeat(d_skip.astype(F32), SSM_HEAD_DIM).reshape(1, d_inner)

    def whole(shape):
        return pl.BlockSpec(shape, lambda b, c: (0, 0))

    return pl.pallas_call(
        functools.partial(_ssd_kernel, d_inner=d_inner),
        grid=(bsz, nc),
        in_specs=[
            pl.BlockSpec((l, width), lambda b, c: (b * nc + c, 0)),
            pl.BlockSpec((l, LANES), lambda b, c: (b * nc + c, 0)),
            whole((SSM_CONV, conv_dim)), whole((1, conv_dim)), whole((1, LANES)), whole((1, LANES)),
            whole((1, d_inner)), whole((1, d_inner)), whole((l, l)), whole((LANES, d_inner)),
        ],
        out_specs=pl.BlockSpec((l, d_inner), lambda b, c: (b * nc + c, 0)),
        out_shape=jax.ShapeDtypeStruct((t, d_inner), BF16),
        scratch_shapes=[
            pltpu.VMEM((CONV_PAD_ROWS + l, conv_dim), F32),
            pltpu.VMEM((SSM_D_STATE, d_inner), F32),
            pltpu.VMEM((l, d_inner), F32),
        ],
        compiler_params=_params(("arbitrary", "arbitrary")),
        name="ssd_mixer",
    )(zx, dt_raw, conv_w.astype(F32), conv_b.astype(F32).reshape(1, conv_dim), pad_heads(dt_bias),
      pad_heads(a_log), d_skip_full, norm_w.astype(F32).reshape(1, d_inner), tri, expand)


def _sb_kernel(q_ref, k_ref, v_ref, suffix_ref, o_ref, *, scale):
    blk = SB_BLOCK
    nq = q_ref.shape[0] // blk
    suffix01 = suffix_ref[...]
    row = jax.lax.broadcasted_iota(jnp.int32, (blk, blk), 0)
    col = jax.lax.broadcasted_iota(jnp.int32, (blk, blk), 1)
    strictly_causal = col < row

    def blocks(first, count=1):
        start = first * blk
        if not isinstance(start, int):
            start = pl.multiple_of(start, blk)
        return pl.ds(start, count * blk)

    def visit(q, j, carry, acc, mask):
        ks = blocks(j)
        z = jax.lax.dot_general(q, k_ref[ks, :], (((1,), (1,)), ((), ())),
                                preferred_element_type=F32) * scale
        sp = _softplus(z)
        log_beta = z - sp
        log_1mb = -sp
        if mask is not None:
            log_1mb = jnp.where(mask, log_1mb, 0.0)
        hi, lo = _split_bf16(log_1mb, 2)
        sums = _dot(jnp.concatenate([hi, lo], axis=1), suffix01)
        w = jnp.exp(log_beta + sums[:, :blk] + carry)
        if mask is not None:
            w = jnp.where(mask, w, 0.0)
        acc = acc + _dot(w.astype(BF16), v_ref[ks, :])
        return carry + sums[:, blk:], acc

    def q_slice(i):
        return blocks(i)

    def finish(i, q, j, carry, acc):
        def cond(state):
            j, carry, _ = state
            return jnp.logical_and(j >= 0, jnp.max(carry) > SB_ZERO_LOG)

        def body(state):
            j, carry, acc = state
            carry, acc = visit(q, j, carry, acc, None)
            return j - 1, carry, acc

        _, _, acc = jax.lax.while_loop(cond, body, (jnp.int32(j), carry, acc))
        o_ref[q_slice(i), :] = acc.astype(o_ref.dtype)

    def single(i):
        q = q_ref[q_slice(i), :]
        zeros = jnp.zeros((blk, blk), F32)
        carry, acc = visit(q, i, zeros, zeros, strictly_causal)
        finish(i, q, i - 1, carry, acc)

    last = (SB_WINDOW - 1) * blk

    def window_scores(q, ks):
        z = jax.lax.dot_general(q, k_ref[ks, :], (((1,), (1,)), ((), ())),
                                preferred_element_type=F32) * scale
        sp = _softplus(z)
        log_1mb = jnp.concatenate([-sp[:, :last], jnp.where(strictly_causal, -sp[:, last:], 0.0)], axis=1)
        hi, lo = _split_bf16(log_1mb, 2)
        stacked = jnp.concatenate(
            [jnp.concatenate([hi[:, b * blk:(b + 1) * blk], lo[:, b * blk:(b + 1) * blk]], axis=1)
             for b in range(SB_WINDOW)], axis=0)
        return z - sp, stacked

    def window_weights(log_beta, sums):
        later = jnp.zeros((blk, blk), F32)
        excl = [None] * SB_WINDOW
        for b in reversed(range(SB_WINDOW)):
            rows = slice(b * blk, (b + 1) * blk)
            excl[b] = sums[rows, :blk] + later
            later = later + sums[rows, blk:]
        w = jnp.exp(log_beta + jnp.concatenate(excl, axis=1))
        w = jnp.concatenate([w[:, :last], jnp.where(strictly_causal, w[:, last:], 0.0)], axis=1)
        return w.astype(BF16), later

    def group(first, size):
        idx = [first + g for g in range(size)]
        qs = [q_ref[q_slice(i), :] for i in idx]
        kss = [blocks(i - (SB_WINDOW - 1), SB_WINDOW) for i in idx]
        scored = [window_scores(q, ks) for q, ks in zip(qs, kss)]
        sums = [_dot(stacked, suffix01) for _, stacked in scored]
        weighted = [window_weights(log_beta, s) for (log_beta, _), s in zip(scored, sums)]
        accs = [_dot(w, v_ref[ks, :]) for (w, _), ks in zip(weighted, kss)]
        for i, acc in zip(idx, accs):
            o_ref[q_slice(i), :] = acc.astype(o_ref.dtype)
        carries = [carry for _, carry in weighted]

        @pl.when(jnp.max(functools.reduce(jnp.maximum, carries)) > SB_ZERO_LOG)
        def _():
            for i, q, carry, acc in zip(idx, qs, carries, accs):
                finish(i, q, i - SB_WINDOW, carry, acc)

    def full_group(ig, _):
        group(n_single + SB_GROUP * ig, SB_GROUP)
        return 0

    n_single = min(SB_WINDOW - 1, nq)
    n_groups, n_left = divmod(nq - n_single, SB_GROUP)
    for i in range(n_single):
        single(i)
    jax.lax.fori_loop(0, n_groups, full_group, 0)
    if n_left:
        group(nq - n_left, n_left)


def _stick_breaking(qkv, bsz, n_heads):
    t = qkv.shape[0]
    s = t // bsz
    dh = SB_HEAD_DIM
    blk = SB_BLOCK
    j_idx = np.arange(blk)[:, None]
    s_idx = np.arange(blk)[None, :]
    half = np.concatenate([(j_idx > s_idx).astype(np.float32), np.ones((blk, blk), np.float32)], axis=1)
    suffix01 = jnp.asarray(np.concatenate([half, half], axis=0), BF16)

    def head_spec(which):
        return pl.BlockSpec((s, dh), functools.partial(lambda b, h, which: (b, which * n_heads + h), which=which))

    return pl.pallas_call(
        functools.partial(_sb_kernel, scale=float(1.0 / np.sqrt(dh).astype(np.float32))),
        grid=(bsz, n_heads),
        in_specs=[head_spec(0), head_spec(1), head_spec(2),
                  pl.BlockSpec((2 * blk, 2 * blk), lambda b, h: (0, 0))],
        out_specs=pl.BlockSpec((s, dh), lambda b, h: (b, h)),
        out_shape=jax.ShapeDtypeStruct((t, n_heads * dh), BF16),
        compiler_params=_params(("parallel", "parallel")),
        name="stick_breaking",
    )(qkv, qkv, qkv, suffix01)


def _layer(h, p, ffn1_pre_w, ffn1_w1, ffn1_w3, ffn1_w2, ffn1_post_w, mix_pre_w, w_in, conv_w, conv_b,
           dt_bias, a_log, d_skip, ssm_norm_w, w_ssm_out, w_sb_out, w_out, mix_post_w, ffn2_pre_w,
           ffn2_w1, ffn2_w3, ffn2_w2, ffn2_post_w, ple_w_gate, ple_w_proj, ple_norm_w, bsz):
    t, d = h.shape
    d_inner = w_ssm_out.shape[0]
    sb_width = w_sb_out.shape[0]
    n_ssm_heads = d_inner // SSM_HEAD_DIM
    conv_dim = conv_w.shape[1]
    bf = lambda w: w.astype(BF16)

    (xn,) = _rowwise(_norm_fn, [h], [ffn1_pre_w], [BF16], "ffn1_pre_norm")
    f = _ffn(xn, bf(ffn1_w1), bf(ffn1_w3), bf(ffn1_w2), "ffn1")
    h, u = _rowwise(functools.partial(_residual_norm_fn, scale=0.5), [h, f], [ffn1_post_w, mix_pre_w],
                    [F32, BF16], "ffn1_residual_mix_norm")

    off_xbc = d_inner
    off_dt = off_xbc + conv_dim
    off_q = off_dt + n_ssm_heads
    off_gate = off_q + 3 * sb_width
    zx = _mm_f32w(u, w_in, BF16, "proj_z_xbc", _ep_identity, col0=0, n=off_dt)
    w_dt = jnp.zeros((d, LANES), BF16).at[:, :n_ssm_heads].set(bf(w_in[:, off_dt:off_q]))
    dt_raw = _mm([(u, w_dt)], F32, "proj_dt", _ep_identity)
    qkv = _mm_f32w(u, w_in[:, off_q:off_gate], BF16, "proj_qkv", _ep_identity)
    gates = _mm_f32w(u, w_in[:, off_gate:], BF16, "proj_gates", _ep_sigmoid)

    y_ssm = _ssd(zx, dt_raw, conv_w, conv_b, dt_bias, a_log, d_skip, ssm_norm_w, bsz, d_inner)
    y_sb = _stick_breaking(qkv, bsz, sb_width // SB_HEAD_DIM)

    tn = 512
    merged = _mm([(y_ssm, bf(w_ssm_out)), (y_sb, bf(w_sb_out))], BF16, "branch_out_gated", _ep_gated_sum,
                 extras=[(gates, 0), (gates, d // tn)], tm=512, tn=tn)
    mix = _mm_f32w(merged, w_out, BF16, "mix_out", _ep_identity)
    h, xn = _rowwise(functools.partial(_residual_norm_fn, scale=1.0), [h, mix], [mix_post_w, ffn2_pre_w],
                     [F32, BF16], "mix_residual_ffn2_norm")

    f = _ffn(xn, bf(ffn2_w1), bf(ffn2_w3), bf(ffn2_w2), "ffn2")
    h, h_bf = _rowwise(functools.partial(_residual_cast_fn, scale=0.5), [h, f], [ffn2_post_w],
                       [F32, BF16], "ffn2_residual")

    ple = _mm([(bf(p), bf(ple_w_proj)), (h_bf, bf(ple_w_gate))], BF16, "ple_gated", _ep_ple)
    (h,) = _rowwise(functools.partial(_residual_fn, scale=1.0), [h, ple], [ple_norm_w], [F32], "ple_residual")
    return h


def kernel(x, p, ffn1_pre_w, ffn1_w1, ffn1_w3, ffn1_w2, ffn1_post_w, mix_pre_w, w_in, conv_w, conv_b, dt_bias, a_log, d_skip, ssm_norm_w, w_ssm_out, w_sb_out, w_out, mix_post_w, ffn2_pre_w, ffn2_w1, ffn2_w3, ffn2_w2, ffn2_post_w, ple_w_gate, ple_w_proj, ple_norm_w):
    bsz, seqlen, d = x.shape
    depth = p.shape[0]
    h = x.reshape(bsz * seqlen, d)
    per_layer = (ffn1_pre_w, ffn1_w1, ffn1_w3, ffn1_w2, ffn1_post_w, mix_pre_w, w_in, conv_w, conv_b,
                 dt_bias, a_log, d_skip, ssm_norm_w, w_ssm_out, w_sb_out, w_out, mix_post_w, ffn2_pre_w,
                 ffn2_w1, ffn2_w3, ffn2_w2, ffn2_post_w, ple_w_gate, ple_w_proj, ple_norm_w)
    for i in range(depth):
        h = _layer(h, p[i].reshape(bsz * seqlen, -1), *[w[i] for w in per_layer], bsz=bsz)
    return h.reshape(bsz, seqlen, d)
```

```python
import functools
import math

import jax
import jax.numpy as jnp
import numpy as np
from jax.experimental import pallas as pl
from jax.experimental.pallas import tpu as pltpu

F32 = jnp.float32
BF16 = jnp.bfloat16

NORM_EPS = 1e-6
SSM_HEAD_DIM = 64
SSM_GROUPS = 8
SSM_D_STATE = 128
SSM_CONV = 4
SSM_CHUNK = 128
SSD_SIDE_ROW_CHUNKS = 4
SB_HEAD_DIM = 128
SB_BLOCK = 128
SB_WINDOW = 3
SB_GROUP = 4
LANES = 128
MXU_COLS = 256
BF16_SUBLANES = 16
CONV_PAD_ROWS = 8
VMEM_LIMIT = 56 * 1024 * 1024
VMEM_LIMIT_LARGE = 60 * 1024 * 1024
SB_ZERO_LOG = -105.0


def _params(semantics, vmem_limit=VMEM_LIMIT):
    return pltpu.CompilerParams(dimension_semantics=semantics, vmem_limit_bytes=vmem_limit)


def _split_bf16(x, parts):
    out = []
    rem = x
    for _ in range(parts):
        piece = rem.astype(BF16)
        out.append(piece)
        rem = rem - piece.astype(F32)
    return out


def _dot(a, b):
    return jnp.dot(a, b, preferred_element_type=F32)


def _dot_01(x, m01, parts, m01_on_left=False):
    acc = None
    for piece in _split_bf16(x, parts):
        term = _dot(m01, piece) if m01_on_left else _dot(piece, m01)
        acc = term if acc is None else acc + term
    return acc


def _softplus(x):
    return jnp.maximum(x, 0.0) + jnp.log1p(jnp.exp(-jnp.abs(x)))


def _silu(x):
    return x * jax.nn.sigmoid(x)


def _rms(x, w):
    x = x.astype(F32)
    return x * jax.lax.rsqrt(jnp.mean(x * x, axis=-1, keepdims=True) + NORM_EPS) * w


def _rowwise_kernel(*refs, fn, n_rows, n_vecs):
    rows = [r[...] for r in refs[:n_rows]]
    vecs = [r[...] for r in refs[n_rows:n_rows + n_vecs]]
    outs = fn(rows, vecs)
    for o_ref, o in zip(refs[n_rows + n_vecs:], outs, strict=True):
        o_ref[...] = o.astype(o_ref.dtype)


def _rowwise(fn, rows, vecs, out_dtypes, name, tm=256):
    t, d = rows[0].shape
    tm = min(tm, t)
    row_spec = pl.BlockSpec((tm, d), lambda i: (i, 0))
    vec_spec = pl.BlockSpec((1, d), lambda i: (0, 0))
    return pl.pallas_call(
        functools.partial(_rowwise_kernel, fn=fn, n_rows=len(rows), n_vecs=len(vecs)),
        grid=(t // tm,),
        in_specs=[row_spec] * len(rows) + [vec_spec] * len(vecs),
        out_specs=[row_spec] * len(out_dtypes),
        out_shape=[jax.ShapeDtypeStruct((t, d), dt) for dt in out_dtypes],
        compiler_params=_params(("parallel",)),
        name=name,
    )(*rows, *[v.reshape(1, d) for v in vecs])


def _norm_fn(rows, vecs):
    return (_rms(rows[0], vecs[0]),)


def _residual_norm_fn(rows, vecs, *, scale):
    h = rows[0] + scale * _rms(rows[1], vecs[0])
    return h, _rms(h, vecs[1])


def _residual_cast_fn(rows, vecs, *, scale):
    h = rows[0] + scale * _rms(rows[1], vecs[0])
    return h, h


def _residual_fn(rows, vecs, *, scale):
    return (rows[0] + scale * _rms(rows[1], vecs[0]),)


def _ffn_kernel(xn_ref, w1_ref, w3_ref, w2_ref, o_ref, acc_ref):
    @pl.when(pl.program_id(1) == 0)
    def _():
        acc_ref[...] = jnp.zeros_like(acc_ref)

    x = xn_ref[...]
    h1 = _dot(x, w1_ref[...])
    h3 = _dot(x, w3_ref[...])
    g = (_silu(h1) * h3).astype(BF16)
    acc_ref[...] += _dot(g, w2_ref[...])

    @pl.when(pl.program_id(1) == pl.num_programs(1) - 1)
    def _():
        o_ref[...] = acc_ref[...].astype(o_ref.dtype)


def _ffn(xn, w1, w3, w2, name, tm=1024, tf=256):
    t, d = xn.shape
    f = w1.shape[1]
    tm = min(tm, t)
    return pl.pallas_call(
        _ffn_kernel,
        grid=(t // tm, f // tf),
        in_specs=[
            pl.BlockSpec((tm, d), lambda i, j: (i, 0)),
            pl.BlockSpec((d, tf), lambda i, j: (0, j)),
            pl.BlockSpec((d, tf), lambda i, j: (0, j)),
            pl.BlockSpec((tf, d), lambda i, j: (j, 0)),
        ],
        out_specs=pl.BlockSpec((tm, d), lambda i, j: (i, 0), pipeline_mode=pl.Buffered(1)),
        out_shape=jax.ShapeDtypeStruct((t, d), BF16),
        scratch_shapes=[pltpu.VMEM((tm, d), F32)],
        compiler_params=_params(("parallel", "arbitrary"), VMEM_LIMIT_LARGE),
        name=name,
    )(xn, w1, w3, w2)


def _mm_kernel(*refs, n_pairs, epilogue):
    a_refs = refs[:n_pairs]
    w_refs = refs[n_pairs:2 * n_pairs]
    e_refs = refs[2 * n_pairs:-1]
    o_ref = refs[-1]
    accs = [_dot(a[...], w[...]) for a, w in zip(a_refs, w_refs, strict=True)]
    o_ref[...] = epilogue(accs, [e[...] for e in e_refs]).astype(o_ref.dtype)


def _mm(pairs, out_dtype, name, epilogue, extras=(), tm=1024, tn=1024):
    t = pairs[0][0].shape[0]
    n = pairs[0][1].shape[1]
    tn = min(tn, n)
    tm = min(tm, t)
    a_specs = [pl.BlockSpec((tm, a.shape[1]), lambda i, j: (i, 0)) for a, _ in pairs]
    w_specs = [pl.BlockSpec((w.shape[0], tn), lambda i, j: (0, j)) for _, w in pairs]
    e_specs = [pl.BlockSpec((tm, tn), functools.partial(lambda i, j, off: (i, j + off), off=off))
               for _, off in extras]
    return pl.pallas_call(
        functools.partial(_mm_kernel, n_pairs=len(pairs), epilogue=epilogue),
        grid=(t // tm, n // tn),
        in_specs=a_specs + w_specs + e_specs,
        out_specs=pl.BlockSpec((tm, tn), lambda i, j: (i, j)),
        out_shape=jax.ShapeDtypeStruct((t, n), out_dtype),
        compiler_params=_params(("parallel", "parallel")),
        name=name,
    )(*[a for a, _ in pairs], *[w for _, w in pairs], *[e for e, _ in extras])


def _ep_identity(accs, extras):
    return accs[0]


def _ep_gated_add(accs, extras):
    return extras[0].astype(F32) * accs[0] + extras[1].astype(F32)


def _ep_sigmoid(accs, extras):
    return jax.nn.sigmoid(accs[0])


def _ep_ple(accs, extras):
    return accs[0] * jax.nn.sigmoid(accs[1])


def _ssd_kernel(zx_ref, dt_ref, convw_ref, convb_ref, dtb_ref, alog_ref, dskip_ref, normw_ref,
                tri_ref, expand_ref, side_a_ref, side_w_ref, side_g_ref, o_ref, side_o_ref,
                ext_ref, state_ref, y_ref, *, d_inner):
    l = SSM_CHUNK
    n = SSM_D_STATE
    gw = d_inner // SSM_GROUPS
    heads_per_group = gw // SSM_HEAD_DIM
    off_b = d_inner
    off_c = d_inner + SSM_GROUPS * n

    @pl.when(pl.program_id(1) == 0)
    def _():
        state_ref[...] = jnp.zeros_like(state_ref)
        ext_ref[0:CONV_PAD_ROWS, :] = jnp.zeros((CONV_PAD_ROWS, ext_ref.shape[1]), F32)

    ext_ref[CONV_PAD_ROWS:CONV_PAD_ROWS + l, :] = zx_ref[:, d_inner:].astype(F32)

    def conv_silu(start, width):
        cols = slice(start, start + width)
        acc = convb_ref[:, cols]
        for k in range(SSM_CONV):
            row0 = CONV_PAD_ROWS - (SSM_CONV - 1) + k
            acc = acc + convw_ref[k:k + 1, cols] * ext_ref[row0:row0 + l, cols]
        return _silu(acc)

    dt = _softplus(dt_ref[...] + dtb_ref[...])
    da = dt * (-jnp.exp(alog_ref[...]))
    a_cum = _dot_01(da, tri_ref[...], 3, m01_on_left=True)
    a_cum_t = a_cum.T
    out_decay = jnp.exp(a_cum)
    state_decay = jnp.exp(a_cum[l - 1:l, :] - a_cum)
    expand = expand_ref[...]
    dt_full = _dot_01(dt, expand, 2)
    out_decay_full = _dot_01(out_decay, expand, 2)
    state_decay_full = _dot_01(state_decay, expand, 2)

    row = jax.lax.broadcasted_iota(jnp.int32, (l, l), 0)
    col = jax.lax.broadcasted_iota(jnp.int32, (l, l), 1)
    causal = col <= row
    lane = jax.lax.broadcasted_iota(jnp.int32, (l, 2 * SSM_HEAD_DIM), 1)
    first_head = lane < SSM_HEAD_DIM

    side_slabs = side_o_ref.shape[1] // MXU_COLS

    for g in range(SSM_GROUPS):
        if g % (SSM_GROUPS // side_slabs) == 0:
            slab = g // (SSM_GROUPS // side_slabs)
            sc = slice(slab * MXU_COLS, (slab + 1) * MXU_COLS)
            side_o_ref[:, sc] = (side_g_ref[:, sc].astype(F32) * _dot(side_a_ref[...], side_w_ref[:, sc])
                                 ).astype(side_o_ref.dtype)
        ch = slice(g * gw, (g + 1) * gw)
        xs = conv_silu(g * gw, gw)
        bm = conv_silu(off_b + g * n, n)
        cm = conv_silu(off_c + g * n, n).astype(BF16)
        bm_t = bm.T.astype(BF16)
        xdt = xs * dt_full[:, ch]
        cb = _dot(cm, bm_t)

        y_parts = []
        for pair in range(heads_per_group // 2):
            gs = []
            for r in (2 * pair, 2 * pair + 1):
                h = g * heads_per_group + r
                diff = a_cum[:, h:h + 1] - a_cum_t[h:h + 1, :]
                decay = jnp.exp(jnp.where(causal, diff, -1e30))
                gs.append((cb * decay).astype(BF16))
            x_pair = xdt[:, pair * 2 * SSM_HEAD_DIM:(pair + 1) * 2 * SSM_HEAD_DIM]
            block_diag = jnp.concatenate(
                [jnp.where(first_head, x_pair, 0.0).astype(BF16),
                 jnp.where(first_head, 0.0, x_pair).astype(BF16)], axis=0)
            y_parts.append(_dot(jnp.concatenate(gs, axis=1), block_diag))
        y_diag = jnp.concatenate(y_parts, axis=1)

        prev = state_ref[:, ch]
        y_off = _dot(cm, prev.astype(BF16)) * out_decay_full[:, ch]
        new_states = _dot(bm_t, (xdt * state_decay_full[:, ch]).astype(BF16))
        state_ref[:, ch] = prev * out_decay_full[l - 1:l, ch] + new_states

        y = y_diag + y_off + dskip_ref[:, ch] * xs
        y_ref[:, ch] = y * _silu(zx_ref[:, ch].astype(F32))

    ext_ref[0:CONV_PAD_ROWS, :] = ext_ref[l:l + CONV_PAD_ROWS, :]
    o_ref[...] = _rms(y_ref[...], normw_ref[...]).astype(o_ref.dtype)


def _ssd(zx, dt_raw, conv_w, conv_b, dt_bias, a_log, d_skip, norm_w, bsz, d_inner, side_a, side_w, side_g,
         side_g_col0):
    side_tm = SSD_SIDE_ROW_CHUNKS * SSM_CHUNK
    side_k, side_n = side_w.shape
    side_tn = side_n // SSD_SIDE_ROW_CHUNKS
    side_g_off = side_g_col0 // side_tn

    def side_tile(b, c):
        step = b * nc + c
        return step // SSD_SIDE_ROW_CHUNKS, step % SSD_SIDE_ROW_CHUNKS

    t, width = zx.shape
    conv_dim = width - d_inner
    n_heads = d_inner // SSM_HEAD_DIM
    nc = t // bsz // SSM_CHUNK
    l = SSM_CHUNK

    def pad_heads(v):
        return jnp.zeros((1, LANES), F32).at[0, :n_heads].set(v.astype(F32))

    tri = jnp.asarray(np.tril(np.ones((l, l), np.float32)), BF16)
    expand = np.zeros((LANES, d_inner), np.float32)
    expand[np.arange(d_inner) // SSM_HEAD_DIM, np.arange(d_inner)] = 1.0
    expand = jnp.asarray(expand, BF16)
    d_skip_full = jnp.repeat(d_skip.astype(F32), SSM_HEAD_DIM).reshape(1, d_inner)

    def whole(shape):
        return pl.BlockSpec(shape, lambda b, c: (0, 0))

    return pl.pallas_call(
        functools.partial(_ssd_kernel, d_inner=d_inner),
        grid=(bsz, nc),
        in_specs=[
            pl.BlockSpec((l, width), lambda b, c: (b * nc + c, 0)),
            pl.BlockSpec((l, LANES), lambda b, c: (b * nc + c, 0)),
            whole((SSM_CONV, conv_dim)), whole((1, conv_dim)), whole((1, LANES)), whole((1, LANES)),
            whole((1, d_inner)), whole((1, d_inner)), whole((l, l)), whole((LANES, d_inner)),
            pl.BlockSpec((side_tm, side_k), lambda b, c: (side_tile(b, c)[0], 0)),
            pl.BlockSpec((side_k, side_tn), lambda b, c: (0, side_tile(b, c)[1])),
            pl.BlockSpec((side_tm, side_tn), lambda b, c: (side_tile(b, c)[0], side_tile(b, c)[1] + side_g_off)),
        ],
        out_specs=[pl.BlockSpec((l, d_inner), lambda b, c: (b * nc + c, 0)),
                   pl.BlockSpec((side_tm, side_tn), side_tile)],
        out_shape=[jax.ShapeDtypeStruct((t, d_inner), BF16), jax.ShapeDtypeStruct((t, side_n), BF16)],
        scratch_shapes=[
            pltpu.VMEM((CONV_PAD_ROWS + l, conv_dim), F32),
            pltpu.VMEM((SSM_D_STATE, d_inner), F32),
            pltpu.VMEM((l, d_inner), F32),
        ],
        compiler_params=_params(("arbitrary", "arbitrary"), VMEM_LIMIT_LARGE),
        name="ssd_mixer",
    )(zx, dt_raw, conv_w.astype(F32), conv_b.astype(F32).reshape(1, conv_dim), pad_heads(dt_bias),
      pad_heads(a_log), d_skip_full, norm_w.astype(F32).reshape(1, d_inner), tri, expand,
      side_a, side_w, side_g)


def _sb_kernel(q_ref, k_ref, v_ref, suffix_ref, *refs, scale, n_cast):
    cast_in, o_ref, cast_out = refs[:n_cast], refs[n_cast], refs[n_cast + 1:]
    for w_ref, wbf_ref in zip(cast_in, cast_out, strict=True):
        wbf_ref[...] = w_ref[...].astype(BF16)

    blk = SB_BLOCK
    nq = q_ref.shape[0] // blk
    suffix01 = suffix_ref[...]
    row = jax.lax.broadcasted_iota(jnp.int32, (blk, blk), 0)
    col = jax.lax.broadcasted_iota(jnp.int32, (blk, blk), 1)
    strictly_causal = col < row

    def blocks(first, count=1):
        start = first * blk
        if not isinstance(start, int):
            start = pl.multiple_of(start, blk)
        return pl.ds(start, count * blk)

    def visit(q, j, carry, acc, mask):
        ks = blocks(j)
        z = jax.lax.dot_general(q, k_ref[ks, :], (((1,), (1,)), ((), ())),
                                preferred_element_type=F32) * scale
        sp = _softplus(z)
        log_beta = z - sp
        log_1mb = -sp
        if mask is not None:
            log_1mb = jnp.where(mask, log_1mb, 0.0)
        hi, lo = _split_bf16(log_1mb, 2)
        sums = _dot(jnp.concatenate([hi, lo], axis=1), suffix01)
        w = jnp.exp(log_beta + sums[:, :blk] + carry)
        if mask is not None:
            w = jnp.where(mask, w, 0.0)
        acc = acc + _dot(w.astype(BF16), v_ref[ks, :])
        return carry + sums[:, blk:], acc

    def q_slice(i):
        return blocks(i)

    def finish(i, q, j, carry, acc):
        def cond(state):
            j, carry, _ = state
            return jnp.logical_and(j >= 0, jnp.max(carry) > SB_ZERO_LOG)

        def body(state):
            j, carry, acc = state
            carry, acc = visit(q, j, carry, acc, None)
            return j - 1, carry, acc

        _, _, acc = jax.lax.while_loop(cond, body, (jnp.int32(j), carry, acc))
        o_ref[q_slice(i), :] = acc.astype(o_ref.dtype)

    def single(i):
        q = q_ref[q_slice(i), :]
        zeros = jnp.zeros((blk, blk), F32)
        carry, acc = visit(q, i, zeros, zeros, strictly_causal)
        finish(i, q, i - 1, carry, acc)

    last = (SB_WINDOW - 1) * blk

    def window_scores(q, ks):
        z = jax.lax.dot_general(q, k_ref[ks, :], (((1,), (1,)), ((), ())),
                                preferred_element_type=F32) * scale
        sp = _softplus(z)
        log_1mb = jnp.concatenate([-sp[:, :last], jnp.where(strictly_causal, -sp[:, last:], 0.0)], axis=1)
        hi, lo = _split_bf16(log_1mb, 2)
        stacked = jnp.concatenate(
            [jnp.concatenate([hi[:, b * blk:(b + 1) * blk], lo[:, b * blk:(b + 1) * blk]], axis=1)
             for b in range(SB_WINDOW)], axis=0)
        return z - sp, stacked

    def window_weights(log_beta, sums):
        later = jnp.zeros((blk, blk), F32)
        excl = [None] * SB_WINDOW
        for b in reversed(range(SB_WINDOW)):
            rows = slice(b * blk, (b + 1) * blk)
            excl[b] = sums[rows, :blk] + later
            later = later + sums[rows, blk:]
        w = jnp.exp(log_beta + jnp.concatenate(excl, axis=1))
        w = jnp.concatenate([w[:, :last], jnp.where(strictly_causal, w[:, last:], 0.0)], axis=1)
        return w.astype(BF16), later

    def group(first, size):
        idx = [first + g for g in range(size)]
        qs = [q_ref[q_slice(i), :] for i in idx]
        kss = [blocks(i - (SB_WINDOW - 1), SB_WINDOW) for i in idx]
        scored = [window_scores(q, ks) for q, ks in zip(qs, kss)]
        sums = [_dot(stacked, suffix01) for _, stacked in scored]
        weighted = [window_weights(log_beta, s) for (log_beta, _), s in zip(scored, sums)]
        accs = [_dot(w, v_ref[ks, :]) for (w, _), ks in zip(weighted, kss)]
        for i, acc in zip(idx, accs):
            o_ref[q_slice(i), :] = acc.astype(o_ref.dtype)
        carries = [carry for _, carry in weighted]

        @pl.when(jnp.max(functools.reduce(jnp.maximum, carries)) > SB_ZERO_LOG)
        def _():
            for i, q, carry, acc in zip(idx, qs, carries, accs):
                finish(i, q, i - SB_WINDOW, carry, acc)

    def full_group(ig, _):
        group(n_single + SB_GROUP * ig, SB_GROUP)
        return 0

    n_single = min(SB_WINDOW - 1, nq)
    n_groups, n_left = divmod(nq - n_single, SB_GROUP)
    for i in range(n_single):
        single(i)
    jax.lax.fori_loop(0, n_groups, full_group, 0)
    if n_left:
        group(nq - n_left, n_left)


def _cast_row_block(rows, steps):
    block = BF16_SUBLANES
    while rows % block or rows // block > steps:
        block += BF16_SUBLANES
        assert block <= rows, (rows, steps)
    return block


def _stick_breaking(qkv, bsz, n_heads, f32_weights):
    steps = bsz * n_heads
    cast_blocks = [_cast_row_block(w.shape[0], steps) for w in f32_weights]

    def cast_spec(w, block):
        last = w.shape[0] // block - 1
        return pl.BlockSpec((block, w.shape[1]), lambda b, h: (jnp.minimum(b * n_heads + h, last), 0))

    cast_specs = [cast_spec(w, blk_rows) for w, blk_rows in zip(f32_weights, cast_blocks)]
    t = qkv.shape[0]
    s = t // bsz
    dh = SB_HEAD_DIM
    blk = SB_BLOCK
    j_idx = np.arange(blk)[:, None]
    s_idx = np.arange(blk)[None, :]
    half = np.concatenate([(j_idx > s_idx).astype(np.float32), np.ones((blk, blk), np.float32)], axis=1)
    suffix01 = jnp.asarray(np.concatenate([half, half], axis=0), BF16)

    def head_spec(which):
        return pl.BlockSpec((s, dh), functools.partial(lambda b, h, which: (b, which * n_heads + h), which=which))

    out, *bf16_weights = pl.pallas_call(
        functools.partial(_sb_kernel, scale=float(1.0 / np.sqrt(dh).astype(np.float32)),
                          n_cast=len(f32_weights)),
        grid=(bsz, n_heads),
        in_specs=[head_spec(0), head_spec(1), head_spec(2),
                  pl.BlockSpec((2 * blk, 2 * blk), lambda b, h: (0, 0))] + cast_specs,
        out_specs=[pl.BlockSpec((s, dh), lambda b, h: (b, h))] + cast_specs,
        out_shape=[jax.ShapeDtypeStruct((t, n_heads * dh), BF16)]
                  + [jax.ShapeDtypeStruct(w.shape, BF16) for w in f32_weights],
        compiler_params=_params(("arbitrary", "arbitrary")),
        name="stick_breaking",
    )(qkv, qkv, qkv, suffix01, *f32_weights)
    return out, bf16_weights


def _layer(h, p, ffn1_pre_w, ffn1_w1, ffn1_w3, ffn1_w2, ffn1_post_w, mix_pre_w, w_in, conv_w, conv_b,
           dt_bias, a_log, d_skip, ssm_norm_w, w_ssm_out, w_sb_out, w_out, mix_post_w, ffn2_pre_w,
           ffn2_w1, ffn2_w3, ffn2_w2, ffn2_post_w, ple_w_gate, ple_w_proj, ple_norm_w, bsz):
    t, d = h.shape
    d_inner = w_ssm_out.shape[0]
    sb_width = w_sb_out.shape[0]
    n_ssm_heads = d_inner // SSM_HEAD_DIM
    conv_dim = conv_w.shape[1]
    bf = lambda w: w.astype(BF16)

    (xn,) = _rowwise(_norm_fn, [h], [ffn1_pre_w], [BF16], "ffn1_pre_norm")
    f = _ffn(xn, bf(ffn1_w1), bf(ffn1_w3), bf(ffn1_w2), "ffn1")
    h, u = _rowwise(functools.partial(_residual_norm_fn, scale=0.5), [h, f], [ffn1_post_w, mix_pre_w],
                    [F32, BF16], "ffn1_residual_mix_norm")

    off_xbc = d_inner
    off_dt = off_xbc + conv_dim
    off_q = off_dt + n_ssm_heads
    off_gate = off_q + 3 * sb_width
    zx = _mm([(u, bf(w_in[:, :off_dt]))], BF16, "proj_z_xbc", _ep_identity)
    w_dt = jnp.zeros((d, LANES), BF16).at[:, :n_ssm_heads].set(bf(w_in[:, off_dt:off_q]))
    dt_raw = _mm([(u, w_dt)], F32, "proj_dt", _ep_identity)
    qkv = _mm([(u, bf(w_in[:, off_q:off_gate]))], BF16, "proj_qkv", _ep_identity)
    gates = _mm([(u, bf(w_in[:, off_gate:]))], BF16, "proj_gates", _ep_sigmoid)

    y_sb, ffn2_bf16 = _stick_breaking(qkv, bsz, sb_width // SB_HEAD_DIM, [ffn2_w1, ffn2_w3, ffn2_w2])
    y_ssm, sb_branch = _ssd(zx, dt_raw, conv_w, conv_b, dt_bias, a_log, d_skip, ssm_norm_w, bsz, d_inner,
                            side_a=y_sb, side_w=bf(w_sb_out), side_g=gates, side_g_col0=d)
    merged = _mm([(y_ssm, bf(w_ssm_out))], BF16, "ssm_out_merge", _ep_gated_add,
                 extras=[(gates, 0), (sb_branch, 0)])
    mix = _mm([(merged, bf(w_out))], BF16, "mix_out", _ep_identity)
    h, xn = _rowwise(functools.partial(_residual_norm_fn, scale=1.0), [h, mix], [mix_post_w, ffn2_pre_w],
                     [F32, BF16], "mix_residual_ffn2_norm")

    f = _ffn(xn, *ffn2_bf16, "ffn2")
    h, h_bf = _rowwise(functools.partial(_residual_cast_fn, scale=0.5), [h, f], [ffn2_post_w],
                       [F32, BF16], "ffn2_residual")

    ple = _mm([(bf(p), bf(ple_w_proj)), (h_bf, bf(ple_w_gate))], BF16, "ple_gated", _ep_ple)
    (h,) = _rowwise(functools.partial(_residual_fn, scale=1.0), [h, ple], [ple_norm_w], [F32], "ple_residual")
    return h


def kernel(x, p, ffn1_pre_w, ffn1_w1, ffn1_w3, ffn1_w2, ffn1_post_w, mix_pre_w, w_in, conv_w, conv_b, dt_bias, a_log, d_skip, ssm_norm_w, w_ssm_out, w_sb_out, w_out, mix_post_w, ffn2_pre_w, ffn2_w1, ffn2_w3, ffn2_w2, ffn2_post_w, ple_w_gate, ple_w_proj, ple_norm_w):
    bsz, seqlen, d = x.shape
    depth = p.shape[0]
    h = x.reshape(bsz * seqlen, d)
    per_layer = (ffn1_pre_w, ffn1_w1, ffn1_w3, ffn1_w2, ffn1_post_w, mix_pre_w, w_in, conv_w, conv_b,
                 dt_bias, a_log, d_skip, ssm_norm_w, w_ssm_out, w_sb_out, w_out, mix_post_w, ffn2_pre_w,
                 ffn2_w1, ffn2_w3, ffn2_w2, ffn2_post_w, ple_w_gate, ple_w_proj, ple_norm_w)
    for i in range(depth):
        h = _layer(h, p[i].reshape(bsz * seqlen, -1), *[w[i] for w in per_layer], bsz=bsz)
    return h.reshape(bsz, seqlen, d)
```

```python
import functools
from typing import NamedTuple

import jax
import jax.numpy as jnp
import numpy as np
from jax.experimental import pallas as pl
from jax.experimental.pallas import tpu as pltpu

F32 = jnp.float32
BF16 = jnp.bfloat16

NORM_EPS = 1e-6
SSM_HEAD_DIM = 64
SSM_GROUPS = 8
SSM_D_STATE = 128
SSM_CONV = 4
SSM_CHUNK = 128
SSD_SIDE_ROW_CHUNKS = 4
SB_HEAD_DIM = 128
SB_BLOCK = 128
SB_WINDOW = 3
SB_GROUP = 4
LANES = 128
MXU_COLS = 256
BF16_SUBLANES = 16
CONV_PAD_ROWS = 8
VMEM_LIMIT = 56 * 1024 * 1024
VMEM_LIMIT_LARGE = 60 * 1024 * 1024
SB_ZERO_LOG = -105.0


def _params(semantics, vmem_limit=VMEM_LIMIT):
    return pltpu.CompilerParams(dimension_semantics=semantics, vmem_limit_bytes=vmem_limit)


def _split_bf16(x, parts):
    out = []
    rem = x
    for _ in range(parts):
        piece = rem.astype(BF16)
        out.append(piece)
        rem = rem - piece.astype(F32)
    return out


def _dot(a, b):
    return jnp.dot(a, b, preferred_element_type=F32)


def _dot_01(x, m01, parts, m01_on_left=False):
    acc = None
    for piece in _split_bf16(x, parts):
        term = _dot(m01, piece) if m01_on_left else _dot(piece, m01)
        acc = term if acc is None else acc + term
    return acc


def _softplus(x):
    return jnp.maximum(x, 0.0) + jnp.log1p(jnp.exp(-jnp.abs(x)))


def _silu(x):
    return x * jax.nn.sigmoid(x)


def _rms(x, w):
    x = x.astype(F32)
    return x * jax.lax.rsqrt(jnp.mean(x * x, axis=-1, keepdims=True) + NORM_EPS) * w


class _Rider(NamedTuple):
    w: jax.Array
    row0: int = 0
    rows: int | None = None

    def n_rows(self):
        return self.w.shape[0] - self.row0 if self.rows is None else self.rows


def _cast_row_block(row0, rows, steps):
    block = BF16_SUBLANES
    while rows % block or row0 % block or rows // block > steps:
        block += BF16_SUBLANES
        assert block <= rows, (row0, rows, steps)
    return block


def _rider_specs(riders, n_steps, step_of):
    in_specs, out_specs, out_shapes = [], [], []
    for rider in riders:
        rows, cols = rider.n_rows(), rider.w.shape[1]
        block = _cast_row_block(rider.row0, rows, n_steps)

        def index(*grid_idx, first=0, last=rows // block - 1):
            return first + jnp.minimum(step_of(*grid_idx), last), 0

        in_specs.append(pl.BlockSpec((block, cols), functools.partial(index, first=rider.row0 // block)))
        out_specs.append(pl.BlockSpec((block, cols), index))
        out_shapes.append(jax.ShapeDtypeStruct((rows, cols), BF16))
    return in_specs, out_specs, out_shapes


def _run_riders(in_refs, out_refs):
    for w_ref, wbf_ref in zip(in_refs, out_refs, strict=True):
        wbf_ref[...] = w_ref[...].astype(BF16)


def _rowwise_kernel(*refs, fn, n_rows, n_vecs):
    rows = [r[...] for r in refs[:n_rows]]
    vecs = [r[...] for r in refs[n_rows:n_rows + n_vecs]]
    outs = fn(rows, vecs)
    for o_ref, o in zip(refs[n_rows + n_vecs:], outs, strict=True):
        o_ref[...] = o.astype(o_ref.dtype)


def _rowwise(fn, rows, vecs, out_dtypes, name, tm=256):
    t, d = rows[0].shape
    tm = min(tm, t)
    row_spec = pl.BlockSpec((tm, d), lambda i: (i, 0))
    vec_spec = pl.BlockSpec((1, d), lambda i: (0, 0))
    return pl.pallas_call(
        functools.partial(_rowwise_kernel, fn=fn, n_rows=len(rows), n_vecs=len(vecs)),
        grid=(t // tm,),
        in_specs=[row_spec] * len(rows) + [vec_spec] * len(vecs),
        out_specs=[row_spec] * len(out_dtypes),
        out_shape=[jax.ShapeDtypeStruct((t, d), dt) for dt in out_dtypes],
        compiler_params=_params(("parallel",)),
        name=name,
    )(*rows, *[v.reshape(1, d) for v in vecs])


def _norm_fn(rows, vecs):
    return (_rms(rows[0], vecs[0]),)


def _residual_norm_fn(rows, vecs, *, scale):
    h = rows[0] + scale * _rms(rows[1], vecs[0])
    return h, _rms(h, vecs[1])


def _residual_cast_fn(rows, vecs, *, scale):
    h = rows[0] + scale * _rms(rows[1], vecs[0])
    return h, h


def _residual_fn(rows, vecs, *, scale):
    return (rows[0] + scale * _rms(rows[1], vecs[0]),)


def _ffn_kernel(xn_ref, w1_ref, w3_ref, w2_ref, *refs, n_riders):
    rider_in, o_ref, rider_out, acc_ref = refs[:n_riders], refs[n_riders], refs[n_riders + 1:-1], refs[-1]

    @pl.when(pl.program_id(1) == 0)
    def _():
        acc_ref[...] = jnp.zeros_like(acc_ref)

    _run_riders(rider_in, rider_out)
    x = xn_ref[...]
    h1 = _dot(x, w1_ref[...])
    h3 = _dot(x, w3_ref[...])
    g = (_silu(h1) * h3).astype(BF16)
    acc_ref[...] += _dot(g, w2_ref[...])

    @pl.when(pl.program_id(1) == pl.num_programs(1) - 1)
    def _():
        o_ref[...] = acc_ref[...].astype(o_ref.dtype)


def _ffn(xn, w1, w3, w2, name, riders=(), tm=1024, tf=256):
    t, d = xn.shape
    f = w1.shape[1]
    tm = min(tm, t)
    grid = (t // tm, f // tf)
    r_in, r_out, r_shapes = _rider_specs(riders, grid[0] * grid[1], lambda i, j: i * grid[1] + j)
    xn_mode = dict(pipeline_mode=pl.Buffered(1)) if riders else {}
    out, *rider_out = pl.pallas_call(
        functools.partial(_ffn_kernel, n_riders=len(riders)),
        grid=grid,
        in_specs=[
            pl.BlockSpec((tm, d), lambda i, j: (i, 0), **xn_mode),
            pl.BlockSpec((d, tf), lambda i, j: (0, j)),
            pl.BlockSpec((d, tf), lambda i, j: (0, j)),
            pl.BlockSpec((tf, d), lambda i, j: (j, 0)),
        ] + r_in,
        out_specs=[pl.BlockSpec((tm, d), lambda i, j: (i, 0), pipeline_mode=pl.Buffered(1))] + r_out,
        out_shape=[jax.ShapeDtypeStruct((t, d), BF16)] + r_shapes,
        scratch_shapes=[pltpu.VMEM((tm, d), F32)],
        compiler_params=_params(("arbitrary", "arbitrary"), VMEM_LIMIT_LARGE),
        name=name,
    )(xn, w1, w3, w2, *[r.w for r in riders])
    return out, rider_out


def _mm_kernel(*refs, n_pairs, n_extras, n_riders, epilogue, w_transposed):
    a_refs = refs[:n_pairs]
    w_refs = refs[n_pairs:2 * n_pairs]
    e_refs = refs[2 * n_pairs:2 * n_pairs + n_extras]
    rest = refs[2 * n_pairs + n_extras:]
    rider_in, o_ref, rider_out = rest[:n_riders], rest[n_riders], rest[n_riders + 1:]
    _run_riders(rider_in, rider_out)
    contract = (((1,), (1,)), ((), ())) if w_transposed else (((1,), (0,)), ((), ()))
    accs = [jax.lax.dot_general(a[...], w[...], contract, preferred_element_type=F32)
            for a, w in zip(a_refs, w_refs, strict=True)]
    o_ref[...] = epilogue(accs, [e[...] for e in e_refs]).astype(o_ref.dtype)


def _mm(pairs, out_dtype, name, epilogue, extras=(), riders=(), w_transposed=False, tm=1024, tn=1024):
    t = pairs[0][0].shape[0]
    n = pairs[0][1].shape[0 if w_transposed else 1]
    tn = min(tn, n)
    tm = min(tm, t)
    grid = (t // tm, n // tn)
    a_specs = [pl.BlockSpec((tm, a.shape[1]), lambda i, j: (i, 0)) for a, _ in pairs]
    if w_transposed:
        w_specs = [pl.BlockSpec((tn, w.shape[1]), lambda i, j: (j, 0)) for _, w in pairs]
    else:
        w_specs = [pl.BlockSpec((w.shape[0], tn), lambda i, j: (0, j)) for _, w in pairs]
    e_specs = [pl.BlockSpec((tm, tn), functools.partial(lambda i, j, off: (i, j + off), off=off))
               for _, off in extras]
    r_in, r_out, r_shapes = _rider_specs(riders, grid[0] * grid[1], lambda i, j: i * grid[1] + j)
    out, *rider_out = pl.pallas_call(
        functools.partial(_mm_kernel, n_pairs=len(pairs), n_extras=len(extras), n_riders=len(riders),
                          epilogue=epilogue, w_transposed=w_transposed),
        grid=grid,
        in_specs=a_specs + w_specs + e_specs + r_in,
        out_specs=[pl.BlockSpec((tm, tn), lambda i, j: (i, j))] + r_out,
        out_shape=[jax.ShapeDtypeStruct((t, n), out_dtype)] + r_shapes,
        compiler_params=_params(("arbitrary", "arbitrary")),
        name=name,
    )(*[a for a, _ in pairs], *[w for _, w in pairs], *[e for e, _ in extras], *[r.w for r in riders])
    return out, rider_out


def _ep_identity(accs, extras):
    return accs[0]


def _ep_gated_add(accs, extras):
    return extras[0].astype(F32) * accs[0] + extras[1].astype(F32)


def _ep_sigmoid(accs, extras):
    return jax.nn.sigmoid(accs[0])


def _ep_ple(accs, extras):
    return accs[0] * jax.nn.sigmoid(accs[1])


def _ssd_kernel(zx_ref, dt_ref, convw_ref, convb_ref, dtb_ref, alog_ref, dskip_ref, normw_ref,
                tri_ref, expand_ref, side_a_ref, side_w_ref, side_g_ref, o_ref, side_o_ref,
                ext_ref, state_ref, y_ref, *, d_inner):
    l = SSM_CHUNK
    n = SSM_D_STATE
    gw = d_inner // SSM_GROUPS
    heads_per_group = gw // SSM_HEAD_DIM
    off_b = d_inner
    off_c = d_inner + SSM_GROUPS * n

    @pl.when(pl.program_id(1) == 0)
    def _():
        state_ref[...] = jnp.zeros_like(state_ref)
        ext_ref[0:CONV_PAD_ROWS, :] = jnp.zeros((CONV_PAD_ROWS, ext_ref.shape[1]), F32)

    ext_ref[CONV_PAD_ROWS:CONV_PAD_ROWS + l, :] = zx_ref[:, d_inner:].astype(F32)

    def conv_silu(start, width):
        cols = slice(start, start + width)
        acc = convb_ref[:, cols]
        for k in range(SSM_CONV):
            row0 = CONV_PAD_ROWS - (SSM_CONV - 1) + k
            acc = acc + convw_ref[k:k + 1, cols] * ext_ref[row0:row0 + l, cols]
        return _silu(acc)

    dt = _softplus(dt_ref[...] + dtb_ref[...])
    da = dt * (-jnp.exp(alog_ref[...]))
    a_cum = _dot_01(da, tri_ref[...], 3, m01_on_left=True)
    a_cum_t = a_cum.T
    out_decay = jnp.exp(a_cum)
    state_decay = jnp.exp(a_cum[l - 1:l, :] - a_cum)
    expand = expand_ref[...]
    dt_full = _dot_01(dt, expand, 2)
    out_decay_full = _dot_01(out_decay, expand, 2)
    state_decay_full = _dot_01(state_decay, expand, 2)

    row = jax.lax.broadcasted_iota(jnp.int32, (l, l), 0)
    col = jax.lax.broadcasted_iota(jnp.int32, (l, l), 1)
    causal = col <= row
    lane = jax.lax.broadcasted_iota(jnp.int32, (l, 2 * SSM_HEAD_DIM), 1)
    first_head = lane < SSM_HEAD_DIM

    side_slabs = side_o_ref.shape[1] // MXU_COLS

    for g in range(SSM_GROUPS):
        if g % (SSM_GROUPS // side_slabs) == 0:
            slab = g // (SSM_GROUPS // side_slabs)
            sc = slice(slab * MXU_COLS, (slab + 1) * MXU_COLS)
            side_o_ref[:, sc] = (side_g_ref[:, sc].astype(F32) * _dot(side_a_ref[...], side_w_ref[:, sc])
                                 ).astype(side_o_ref.dtype)
        ch = slice(g * gw, (g + 1) * gw)
        xs = conv_silu(g * gw, gw)
        bm = conv_silu(off_b + g * n, n)
        cm = conv_silu(off_c + g * n, n).astype(BF16)
        bm_t = bm.T.astype(BF16)
        xdt = xs * dt_full[:, ch]
        cb = _dot(cm, bm_t)

        y_parts = []
        for pair in range(heads_per_group // 2):
            gs = []
            for r in (2 * pair, 2 * pair + 1):
                h = g * heads_per_group + r
                diff = a_cum[:, h:h + 1] - a_cum_t[h:h + 1, :]
                decay = jnp.exp(jnp.where(causal, diff, -1e30))
                gs.append((cb * decay).astype(BF16))
            x_pair = xdt[:, pair * 2 * SSM_HEAD_DIM:(pair + 1) * 2 * SSM_HEAD_DIM]
            block_diag = jnp.concatenate(
                [jnp.where(first_head, x_pair, 0.0).astype(BF16),
                 jnp.where(first_head, 0.0, x_pair).astype(BF16)], axis=0)
            y_parts.append(_dot(jnp.concatenate(gs, axis=1), block_diag))
        y_diag = jnp.concatenate(y_parts, axis=1)

        prev = state_ref[:, ch]
        y_off = _dot(cm, prev.astype(BF16)) * out_decay_full[:, ch]
        new_states = _dot(bm_t, (xdt * state_decay_full[:, ch]).astype(BF16))
        state_ref[:, ch] = prev * out_decay_full[l - 1:l, ch] + new_states

        y = y_diag + y_off + dskip_ref[:, ch] * xs
        y_ref[:, ch] = y * _silu(zx_ref[:, ch].astype(F32))

    ext_ref[0:CONV_PAD_ROWS, :] = ext_ref[l:l + CONV_PAD_ROWS, :]
    o_ref[...] = _rms(y_ref[...], normw_ref[...]).astype(o_ref.dtype)


def _ssd(zx, dt_raw, conv_w, conv_b, dt_bias, a_log, d_skip, norm_w, bsz, d_inner, side_a, side_w, side_g,
         side_g_col0):
    side_tm = SSD_SIDE_ROW_CHUNKS * SSM_CHUNK
    side_k, side_n = side_w.shape
    side_tn = side_n // SSD_SIDE_ROW_CHUNKS
    side_g_off = side_g_col0 // side_tn

    def side_tile(b, c):
        step = b * nc + c
        return step // SSD_SIDE_ROW_CHUNKS, step % SSD_SIDE_ROW_CHUNKS

    t, width = zx.shape
    conv_dim = width - d_inner
    n_heads = d_inner // SSM_HEAD_DIM
    nc = t // bsz // SSM_CHUNK
    l = SSM_CHUNK

    def pad_heads(v):
        return jnp.zeros((1, LANES), F32).at[0, :n_heads].set(v.astype(F32))

    tri = jnp.asarray(np.tril(np.ones((l, l), np.float32)), BF16)
    expand = np.zeros((LANES, d_inner), np.float32)
    expand[np.arange(d_inner) // SSM_HEAD_DIM, np.arange(d_inner)] = 1.0
    expand = jnp.asarray(expand, BF16)
    d_skip_full = jnp.repeat(d_skip.astype(F32), SSM_HEAD_DIM).reshape(1, d_inner)

    def whole(shape):
        return pl.BlockSpec(shape, lambda b, c: (0, 0))

    return pl.pallas_call(
        functools.partial(_ssd_kernel, d_inner=d_inner),
        grid=(bsz, nc),
        in_specs=[
            pl.BlockSpec((l, width), lambda b, c: (b * nc + c, 0)),
            pl.BlockSpec((l, LANES), lambda b, c: (b * nc + c, 0)),
            whole((SSM_CONV, conv_dim)), whole((1, conv_dim)), whole((1, LANES)), whole((1, LANES)),
            whole((1, d_inner)), whole((1, d_inner)), whole((l, l)), whole((LANES, d_inner)),
            pl.BlockSpec((side_tm, side_k), lambda b, c: (side_tile(b, c)[0], 0)),
            pl.BlockSpec((side_k, side_tn), lambda b, c: (0, side_tile(b, c)[1])),
            pl.BlockSpec((side_tm, side_tn), lambda b, c: (side_tile(b, c)[0], side_tile(b, c)[1] + side_g_off)),
        ],
        out_specs=[pl.BlockSpec((l, d_inner), lambda b, c: (b * nc + c, 0)),
                   pl.BlockSpec((side_tm, side_tn), side_tile)],
        out_shape=[jax.ShapeDtypeStruct((t, d_inner), BF16), jax.ShapeDtypeStruct((t, side_n), BF16)],
        scratch_shapes=[
            pltpu.VMEM((CONV_PAD_ROWS + l, conv_dim), F32),
            pltpu.VMEM((SSM_D_STATE, d_inner), F32),
            pltpu.VMEM((l, d_inner), F32),
        ],
        compiler_params=_params(("arbitrary", "arbitrary"), VMEM_LIMIT_LARGE),
        name="ssd_mixer",
    )(zx, dt_raw, conv_w.astype(F32), conv_b.astype(F32).reshape(1, conv_dim), pad_heads(dt_bias),
      pad_heads(a_log), d_skip_full, norm_w.astype(F32).reshape(1, d_inner), tri, expand,
      side_a, side_w, side_g)


def _sb_kernel(q_ref, k_ref, v_ref, suffix_ref, *refs, scale, n_riders):
    rider_in, o_ref, rider_out = refs[:n_riders], refs[n_riders], refs[n_riders + 1:]
    _run_riders(rider_in, rider_out)

    blk = SB_BLOCK
    nq = q_ref.shape[0] // blk
    suffix01 = suffix_ref[...]
    row = jax.lax.broadcasted_iota(jnp.int32, (blk, blk), 0)
    col = jax.lax.broadcasted_iota(jnp.int32, (blk, blk), 1)
    strictly_causal = col < row

    def blocks(first, count=1):
        start = first * blk
        if not isinstance(start, int):
            start = pl.multiple_of(start, blk)
        return pl.ds(start, count * blk)

    def visit(q, j, carry, acc, mask):
        ks = blocks(j)
        z = jax.lax.dot_general(q, k_ref[ks, :], (((1,), (1,)), ((), ())),
                                preferred_element_type=F32) * scale
        sp = _softplus(z)
        log_beta = z - sp
        log_1mb = -sp
        if mask is not None:
            log_1mb = jnp.where(mask, log_1mb, 0.0)
        hi, lo = _split_bf16(log_1mb, 2)
        sums = _dot(jnp.concatenate([hi, lo], axis=1), suffix01)
        w = jnp.exp(log_beta + sums[:, :blk] + carry)
        if mask is not None:
            w = jnp.where(mask, w, 0.0)
        acc = acc + _dot(w.astype(BF16), v_ref[ks, :])
        return carry + sums[:, blk:], acc

    def q_slice(i):
        return blocks(i)

    def finish(i, q, j, carry, acc):
        def cond(state):
            j, carry, _ = state
            return jnp.logical_and(j >= 0, jnp.max(carry) > SB_ZERO_LOG)

        def body(state):
            j, carry, acc = state
            carry, acc = visit(q, j, carry, acc, None)
            return j - 1, carry, acc

        _, _, acc = jax.lax.while_loop(cond, body, (jnp.int32(j), carry, acc))
        o_ref[q_slice(i), :] = acc.astype(o_ref.dtype)

    def single(i):
        q = q_ref[q_slice(i), :]
        zeros = jnp.zeros((blk, blk), F32)
        carry, acc = visit(q, i, zeros, zeros, strictly_causal)
        finish(i, q, i - 1, carry, acc)

    last = (SB_WINDOW - 1) * blk

    def window_scores(q, ks):
        z = jax.lax.dot_general(q, k_ref[ks, :], (((1,), (1,)), ((), ())),
                                preferred_element_type=F32) * scale
        sp = _softplus(z)
        log_1mb = jnp.concatenate([-sp[:, :last], jnp.where(strictly_causal, -sp[:, last:], 0.0)], axis=1)
        hi, lo = _split_bf16(log_1mb, 2)
        stacked = jnp.concatenate(
            [jnp.concatenate([hi[:, b * blk:(b + 1) * blk], lo[:, b * blk:(b + 1) * blk]], axis=1)
             for b in range(SB_WINDOW)], axis=0)
        return z - sp, stacked

    def window_weights(log_beta, sums):
        later = jnp.zeros((blk, blk), F32)
        excl = [None] * SB_WINDOW
        for b in reversed(range(SB_WINDOW)):
            rows = slice(b * blk, (b + 1) * blk)
            excl[b] = sums[rows, :blk] + later
            later = later + sums[rows, blk:]
        w = jnp.exp(log_beta + jnp.concatenate(excl, axis=1))
        w = jnp.concatenate([w[:, :last], jnp.where(strictly_causal, w[:, last:], 0.0)], axis=1)
        return w.astype(BF16), later

    def group(first, size):
        idx = [first + g for g in range(size)]
        qs = [q_ref[q_slice(i), :] for i in idx]
        kss = [blocks(i - (SB_WINDOW - 1), SB_WINDOW) for i in idx]
        scored = [window_scores(q, ks) for q, ks in zip(qs, kss)]
        sums = [_dot(stacked, suffix01) for _, stacked in scored]
        weighted = [window_weights(log_beta, s) for (log_beta, _), s in zip(scored, sums)]
        accs = [_dot(w, v_ref[ks, :]) for (w, _), ks in zip(weighted, kss)]
        for i, acc in zip(idx, accs):
            o_ref[q_slice(i), :] = acc.astype(o_ref.dtype)
        carries = [carry for _, carry in weighted]

        @pl.when(jnp.max(functools.reduce(jnp.maximum, carries)) > SB_ZERO_LOG)
        def _():
            for i, q, carry, acc in zip(idx, qs, carries, accs):
                finish(i, q, i - SB_WINDOW, carry, acc)

    def full_group(ig, _):
        group(n_single + SB_GROUP * ig, SB_GROUP)
        return 0

    n_single = min(SB_WINDOW - 1, nq)
    n_groups, n_left = divmod(nq - n_single, SB_GROUP)
    for i in range(n_single):
        single(i)
    jax.lax.fori_loop(0, n_groups, full_group, 0)
    if n_left:
        group(nq - n_left, n_left)


def _stick_breaking(qkv, bsz, n_heads, riders=()):
    r_in, r_out, r_shapes = _rider_specs(riders, bsz * n_heads, lambda b, h: b * n_heads + h)
    t = qkv.shape[0]
    s = t // bsz
    dh = SB_HEAD_DIM
    blk = SB_BLOCK
    j_idx = np.arange(blk)[:, None]
    s_idx = np.arange(blk)[None, :]
    half = np.concatenate([(j_idx > s_idx).astype(np.float32), np.ones((blk, blk), np.float32)], axis=1)
    suffix01 = jnp.asarray(np.concatenate([half, half], axis=0), BF16)

    def head_spec(which):
        return pl.BlockSpec((s, dh), functools.partial(lambda b, h, which: (b, which * n_heads + h), which=which))

    out, *rider_out = pl.pallas_call(
        functools.partial(_sb_kernel, scale=float(1.0 / np.sqrt(dh).astype(np.float32)),
                          n_riders=len(riders)),
        grid=(bsz, n_heads),
        in_specs=[head_spec(0), head_spec(1), head_spec(2),
                  pl.BlockSpec((2 * blk, 2 * blk), lambda b, h: (0, 0))] + r_in,
        out_specs=[pl.BlockSpec((s, dh), lambda b, h: (b, h))] + r_out,
        out_shape=[jax.ShapeDtypeStruct((t, n_heads * dh), BF16)] + r_shapes,
        compiler_params=_params(("arbitrary", "arbitrary")),
        name="stick_breaking",
    )(qkv, qkv, qkv, suffix01, *[r.w for r in riders])
    return out, rider_out


def _layer(h, p, ffn1_pre_w, ffn1_w1, ffn1_w3, ffn1_w2, ffn1_post_w, mix_pre_w, w_in, conv_w, conv_b,
           dt_bias, a_log, d_skip, ssm_norm_w, w_ssm_out, w_sb_out, w_out, mix_post_w, ffn2_pre_w,
           ffn2_w1, ffn2_w3, ffn2_w2, ffn2_post_w, ple_w_gate, ple_w_proj, ple_norm_w, bsz):
    t, d = h.shape
    d_inner = w_ssm_out.shape[0]
    sb_width = w_sb_out.shape[0]
    n_ssm_heads = d_inner // SSM_HEAD_DIM
    conv_dim = conv_w.shape[1]
    bf = lambda w: w.astype(BF16)

    off_xbc = d_inner
    off_dt = off_xbc + conv_dim
    off_q = off_dt + n_ssm_heads
    off_gate = off_q + 3 * sb_width
    w_in_t = w_in.T
    (xn,) = _rowwise(_norm_fn, [h], [ffn1_pre_w], [BF16], "ffn1_pre_norm")
    f, (w_zx_t, w_qkv_t, w_gates_t) = _ffn(
        xn, bf(ffn1_w1), bf(ffn1_w3), bf(ffn1_w2), "ffn1",
        riders=[_Rider(w_in_t, 0, off_dt), _Rider(w_in_t, off_q, off_gate - off_q), _Rider(w_in_t, off_gate)])
    h, u = _rowwise(functools.partial(_residual_norm_fn, scale=0.5), [h, f], [ffn1_post_w, mix_pre_w],
                    [F32, BF16], "ffn1_residual_mix_norm")

    zx, _ = _mm([(u, w_zx_t)], BF16, "proj_z_xbc", _ep_identity, w_transposed=True)
    w_dt_t = jnp.zeros((LANES, d), BF16).at[:n_ssm_heads].set(bf(w_in_t[off_dt:off_q]))
    dt_raw, _ = _mm([(u, w_dt_t)], F32, "proj_dt", _ep_identity, w_transposed=True)
    qkv, (w_ssm_out_bf, w_sb_out_bf, w_out_bf, ple_w_gate_bf) = _mm(
        [(u, w_qkv_t)], BF16, "proj_qkv", _ep_identity, w_transposed=True,
        riders=[_Rider(w_ssm_out), _Rider(w_sb_out), _Rider(w_out), _Rider(ple_w_gate)])
    gates, _ = _mm([(u, w_gates_t)], BF16, "proj_gates", _ep_sigmoid, w_transposed=True)

    y_sb, ffn2_bf16 = _stick_breaking(qkv, bsz, sb_width // SB_HEAD_DIM,
                                      riders=[_Rider(ffn2_w1), _Rider(ffn2_w3), _Rider(ffn2_w2)])
    y_ssm, sb_branch = _ssd(zx, dt_raw, conv_w, conv_b, dt_bias, a_log, d_skip, ssm_norm_w, bsz, d_inner,
                            side_a=y_sb, side_w=w_sb_out_bf, side_g=gates, side_g_col0=d)
    merged, _ = _mm([(y_ssm, w_ssm_out_bf)], BF16, "ssm_out_merge", _ep_gated_add,
                    extras=[(gates, 0), (sb_branch, 0)])
    mix, _ = _mm([(merged, w_out_bf)], BF16, "mix_out", _ep_identity)
    h, xn = _rowwise(functools.partial(_residual_norm_fn, scale=1.0), [h, mix], [mix_post_w, ffn2_pre_w],
                     [F32, BF16], "mix_residual_ffn2_norm")

    f, _ = _ffn(xn, *ffn2_bf16, "ffn2")
    h, h_bf = _rowwise(functools.partial(_residual_cast_fn, scale=0.5), [h, f], [ffn2_post_w],
                       [F32, BF16], "ffn2_residual")

    ple, _ = _mm([(bf(p), bf(ple_w_proj)), (h_bf, ple_w_gate_bf)], BF16, "ple_gated", _ep_ple)
    (h,) = _rowwise(functools.partial(_residual_fn, scale=1.0), [h, ple], [ple_norm_w], [F32], "ple_residual")
    return h


def kernel(x, p, ffn1_pre_w, ffn1_w1, ffn1_w3, ffn1_w2, ffn1_post_w, mix_pre_w, w_in, conv_w, conv_b, dt_bias, a_log, d_skip, ssm_norm_w, w_ssm_out, w_sb_out, w_out, mix_post_w, ffn2_pre_w, ffn2_w1, ffn2_w3, ffn2_w2, ffn2_post_w, ple_w_gate, ple_w_proj, ple_norm_w):
    bsz, seqlen, d = x.shape
    depth = p.shape[0]
    h = x.reshape(bsz * seqlen, d)
    per_layer = (ffn1_pre_w, ffn1_w1, ffn1_w3, ffn1_w2, ffn1_post_w, mix_pre_w, w_in, conv_w, conv_b,
                 dt_bias, a_log, d_skip, ssm_norm_w, w_ssm_out, w_sb_out, w_out, mix_post_w, ffn2_pre_w,
                 ffn2_w1, ffn2_w3, ffn2_w2, ffn2_post_w, ple_w_gate, ple_w_proj, ple_norm_w)
    for i in range(depth):
        h = _layer(h, p[i].reshape(bsz * seqlen, -1), *[w[i] for w in per_layer], bsz=bsz)
    return h.reshape(bsz, seqlen, d)
```

```python
import functools
from typing import NamedTuple

import jax
import jax.numpy as jnp
import numpy as np
from jax.experimental import pallas as pl
from jax.experimental.pallas import tpu as pltpu

F32 = jnp.float32
BF16 = jnp.bfloat16

NORM_EPS = 1e-6
SSM_HEAD_DIM = 64
SSM_GROUPS = 8
SSM_D_STATE = 128
SSM_CONV = 4
SSM_CHUNK = 128
SSD_SIDE_ROW_CHUNKS = 4
SB_HEAD_DIM = 128
SB_BLOCK = 128
SB_WINDOW = 3
SB_GROUP = 8
LANES = 128
MXU_COLS = 256
BF16_SUBLANES = 16
CONV_PAD_ROWS = 8
VMEM_LIMIT = 56 * 1024 * 1024
VMEM_LIMIT_LARGE = 60 * 1024 * 1024
SB_ZERO_LOG = -105.0


def _params(semantics, vmem_limit=VMEM_LIMIT):
    return pltpu.CompilerParams(dimension_semantics=semantics, vmem_limit_bytes=vmem_limit)


def _split_bf16(x, parts):
    out = []
    rem = x
    for _ in range(parts):
        piece = rem.astype(BF16)
        out.append(piece)
        rem = rem - piece.astype(F32)
    return out


def _dot(a, b):
    return jnp.dot(a, b, preferred_element_type=F32)


def _dot_01(x, m01, parts, m01_on_left=False):
    acc = None
    for piece in _split_bf16(x, parts):
        term = _dot(m01, piece) if m01_on_left else _dot(piece, m01)
        acc = term if acc is None else acc + term
    return acc


def _softplus(x):
    return jnp.maximum(x, 0.0) + jnp.log1p(jnp.exp(-jnp.abs(x)))


def _softplus_abs(x):
    return jnp.maximum(x, 0.0) + jnp.log(1.0 + jnp.exp(-jnp.abs(x)))


def _silu(x):
    return x * jax.nn.sigmoid(x)


def _rms(x, w):
    x = x.astype(F32)
    return x * jax.lax.rsqrt(jnp.mean(x * x, axis=-1, keepdims=True) + NORM_EPS) * w


class _Rider(NamedTuple):
    w: jax.Array
    row0: int = 0
    rows: int | None = None

    def n_rows(self):
        return self.w.shape[0] - self.row0 if self.rows is None else self.rows


def _cast_row_block(row0, rows, steps):
    block = BF16_SUBLANES
    while rows % block or row0 % block or rows // block > steps:
        block += BF16_SUBLANES
        assert block <= rows, (row0, rows, steps)
    return block


def _rider_specs(riders, n_steps, step_of):
    in_specs, out_specs, out_shapes = [], [], []
    for rider in riders:
        rows, cols = rider.n_rows(), rider.w.shape[1]
        block = _cast_row_block(rider.row0, rows, n_steps)

        def index(*grid_idx, first=0, last=rows // block - 1):
            return first + jnp.minimum(step_of(*grid_idx), last), 0

        in_specs.append(pl.BlockSpec((block, cols), functools.partial(index, first=rider.row0 // block)))
        out_specs.append(pl.BlockSpec((block, cols), index))
        out_shapes.append(jax.ShapeDtypeStruct((rows, cols), BF16))
    return in_specs, out_specs, out_shapes


def _run_riders(in_refs, out_refs):
    for w_ref, wbf_ref in zip(in_refs, out_refs, strict=True):
        wbf_ref[...] = w_ref[...].astype(BF16)


def _rowwise_kernel(*refs, fn, n_rows, n_vecs):
    rows = [r[...] for r in refs[:n_rows]]
    vecs = [r[...] for r in refs[n_rows:n_rows + n_vecs]]
    outs = fn(rows, vecs)
    for o_ref, o in zip(refs[n_rows + n_vecs:], outs, strict=True):
        o_ref[...] = o.astype(o_ref.dtype)


def _rowwise(fn, rows, vecs, out_dtypes, name, tm=256):
    t, d = rows[0].shape
    tm = min(tm, t)
    row_spec = pl.BlockSpec((tm, d), lambda i: (i, 0))
    vec_spec = pl.BlockSpec((1, d), lambda i: (0, 0))
    return pl.pallas_call(
        functools.partial(_rowwise_kernel, fn=fn, n_rows=len(rows), n_vecs=len(vecs)),
        grid=(t // tm,),
        in_specs=[row_spec] * len(rows) + [vec_spec] * len(vecs),
        out_specs=[row_spec] * len(out_dtypes),
        out_shape=[jax.ShapeDtypeStruct((t, d), dt) for dt in out_dtypes],
        compiler_params=_params(("parallel",)),
        name=name,
    )(*rows, *[v.reshape(1, d) for v in vecs])


def _norm_fn(rows, vecs):
    return (_rms(rows[0], vecs[0]),)


def _residual_norm_fn(rows, vecs, *, scale):
    h = rows[0] + scale * _rms(rows[1], vecs[0])
    return h, _rms(h, vecs[1])


def _residual_cast_fn(rows, vecs, *, scale):
    h = rows[0] + scale * _rms(rows[1], vecs[0])
    return h, h


def _residual_fn(rows, vecs, *, scale):
    return (rows[0] + scale * _rms(rows[1], vecs[0]),)


def _ffn_kernel(xn_ref, w1_ref, w3_ref, w2_ref, *refs, n_riders):
    rider_in, o_ref, rider_out, acc_ref = refs[:n_riders], refs[n_riders], refs[n_riders + 1:-1], refs[-1]

    @pl.when(pl.program_id(1) == 0)
    def _():
        acc_ref[...] = jnp.zeros_like(acc_ref)

    _run_riders(rider_in, rider_out)
    x = xn_ref[...]
    h1 = _dot(x, w1_ref[...])
    h3 = _dot(x, w3_ref[...])
    g = (_silu(h1) * h3).astype(BF16)
    acc_ref[...] += _dot(g, w2_ref[...])

    @pl.when(pl.program_id(1) == pl.num_programs(1) - 1)
    def _():
        o_ref[...] = acc_ref[...].astype(o_ref.dtype)


def _ffn(xn, w1, w3, w2, name, riders=(), tm=1024, tf=256):
    t, d = xn.shape
    f = w1.shape[1]
    tm = min(tm, t)
    grid = (t // tm, f // tf)
    r_in, r_out, r_shapes = _rider_specs(riders, grid[0] * grid[1], lambda i, j: i * grid[1] + j)
    out, *rider_out = pl.pallas_call(
        functools.partial(_ffn_kernel, n_riders=len(riders)),
        grid=grid,
        in_specs=[
            pl.BlockSpec((tm, d), lambda i, j: (i, 0)),
            pl.BlockSpec((d, tf), lambda i, j: (0, j)),
            pl.BlockSpec((d, tf), lambda i, j: (0, j)),
            pl.BlockSpec((tf, d), lambda i, j: (j, 0)),
        ] + r_in,
        out_specs=[pl.BlockSpec((tm, d), lambda i, j: (i, 0), pipeline_mode=pl.Buffered(1))] + r_out,
        out_shape=[jax.ShapeDtypeStruct((t, d), BF16)] + r_shapes,
        scratch_shapes=[pltpu.VMEM((tm, d), F32)],
        compiler_params=_params(("arbitrary", "arbitrary"), VMEM_LIMIT_LARGE),
        name=name,
    )(xn, w1, w3, w2, *[r.w for r in riders])
    return out, rider_out


def _mm_kernel(*refs, n_pairs, n_extras, n_riders, epilogue, w_transposed):
    a_refs = refs[:n_pairs]
    w_refs = refs[n_pairs:2 * n_pairs]
    e_refs = refs[2 * n_pairs:2 * n_pairs + n_extras]
    rest = refs[2 * n_pairs + n_extras:]
    rider_in, o_ref, rider_out = rest[:n_riders], rest[n_riders], rest[n_riders + 1:]
    _run_riders(rider_in, rider_out)
    contract = (((1,), (1,)), ((), ())) if w_transposed else (((1,), (0,)), ((), ()))
    accs = [jax.lax.dot_general(a[...], w[...], contract, preferred_element_type=F32)
            for a, w in zip(a_refs, w_refs, strict=True)]
    o_ref[...] = epilogue(accs, [e[...] for e in e_refs]).astype(o_ref.dtype)


def _mm(pairs, out_dtype, name, epilogue, extras=(), riders=(), w_transposed=False, tm=1024, tn=1024):
    t = pairs[0][0].shape[0]
    n = pairs[0][1].shape[0 if w_transposed else 1]
    tn = min(tn, n)
    tm = min(tm, t)
    grid = (t // tm, n // tn)
    a_specs = [pl.BlockSpec((tm, a.shape[1]), lambda i, j: (i, 0)) for a, _ in pairs]
    if w_transposed:
        w_specs = [pl.BlockSpec((tn, w.shape[1]), lambda i, j: (j, 0)) for _, w in pairs]
    else:
        w_specs = [pl.BlockSpec((w.shape[0], tn), lambda i, j: (0, j)) for _, w in pairs]
    e_specs = [pl.BlockSpec((tm, tn), functools.partial(lambda i, j, off: (i, j + off), off=off))
               for _, off in extras]
    r_in, r_out, r_shapes = _rider_specs(riders, grid[0] * grid[1], lambda i, j: i * grid[1] + j)
    out, *rider_out = pl.pallas_call(
        functools.partial(_mm_kernel, n_pairs=len(pairs), n_extras=len(extras), n_riders=len(riders),
                          epilogue=epilogue, w_transposed=w_transposed),
        grid=grid,
        in_specs=a_specs + w_specs + e_specs + r_in,
        out_specs=[pl.BlockSpec((tm, tn), lambda i, j: (i, j))] + r_out,
        out_shape=[jax.ShapeDtypeStruct((t, n), out_dtype)] + r_shapes,
        compiler_params=_params(("arbitrary", "arbitrary")),
        name=name,
    )(*[a for a, _ in pairs], *[w for _, w in pairs], *[e for e, _ in extras], *[r.w for r in riders])
    return out, rider_out


def _ep_identity(accs, extras):
    return accs[0]


def _ep_gated_add(accs, extras):
    return extras[0].astype(F32) * accs[0] + extras[1].astype(F32)


def _ep_sigmoid(accs, extras):
    return jax.nn.sigmoid(accs[0])


def _ep_ple(accs, extras):
    return accs[0] * jax.nn.sigmoid(accs[1])


def _ssd_kernel(zx_ref, dt_ref, convw_ref, convb_ref, dtb_ref, alog_ref, dskip_ref, normw_ref,
                tri_ref, expand_ref, side_a_ref, side_w_ref, side_g_ref, o_ref, side_o_ref,
                ext_ref, state_ref, y_ref, *, d_inner):
    l = SSM_CHUNK
    n = SSM_D_STATE
    gw = d_inner // SSM_GROUPS
    heads_per_group = gw // SSM_HEAD_DIM
    off_b = d_inner
    off_c = d_inner + SSM_GROUPS * n

    @pl.when(pl.program_id(1) == 0)
    def _():
        state_ref[...] = jnp.zeros_like(state_ref)
        ext_ref[0:CONV_PAD_ROWS, :] = jnp.zeros((CONV_PAD_ROWS, ext_ref.shape[1]), F32)

    ext_ref[CONV_PAD_ROWS:CONV_PAD_ROWS + l, :] = zx_ref[:, d_inner:].astype(F32)

    def conv_silu(start, width):
        cols = slice(start, start + width)
        acc = convb_ref[:, cols]
        for k in range(SSM_CONV):
            row0 = CONV_PAD_ROWS - (SSM_CONV - 1) + k
            acc = acc + convw_ref[k:k + 1, cols] * ext_ref[row0:row0 + l, cols]
        return _silu(acc)

    dt = _softplus(dt_ref[...] + dtb_ref[...])
    da = dt * (-jnp.exp(alog_ref[...]))
    a_cum = _dot_01(da, tri_ref[...], 3, m01_on_left=True)
    a_cum_t = a_cum.T
    out_decay = jnp.exp(a_cum)
    state_decay = jnp.exp(a_cum[l - 1:l, :] - a_cum)
    expand = expand_ref[...]
    dt_full = _dot_01(dt, expand, 2)
    out_decay_full = _dot_01(out_decay, expand, 2)
    state_decay_full = _dot_01(state_decay, expand, 2)

    row = jax.lax.broadcasted_iota(jnp.int32, (l, l), 0)
    col = jax.lax.broadcasted_iota(jnp.int32, (l, l), 1)
    causal = col <= row
    lane = jax.lax.broadcasted_iota(jnp.int32, (l, 2 * SSM_HEAD_DIM), 1)
    first_head = lane < SSM_HEAD_DIM

    side_slabs = side_o_ref.shape[1] // MXU_COLS

    for g in range(SSM_GROUPS):
        if g % (SSM_GROUPS // side_slabs) == 0:
            slab = g // (SSM_GROUPS // side_slabs)
            sc = slice(slab * MXU_COLS, (slab + 1) * MXU_COLS)
            side_o_ref[:, sc] = (side_g_ref[:, sc].astype(F32) * _dot(side_a_ref[...], side_w_ref[:, sc])
                                 ).astype(side_o_ref.dtype)
        ch = slice(g * gw, (g + 1) * gw)
        xs = conv_silu(g * gw, gw)
        bm = conv_silu(off_b + g * n, n)
        cm = conv_silu(off_c + g * n, n).astype(BF16)
        bm_t = bm.T.astype(BF16)
        xdt = xs * dt_full[:, ch]
        cb = _dot(cm, bm_t)

        y_parts = []
        for pair in range(heads_per_group // 2):
            gs = []
            for r in (2 * pair, 2 * pair + 1):
                h = g * heads_per_group + r
                diff = a_cum[:, h:h + 1] - a_cum_t[h:h + 1, :]
                decay = jnp.exp(jnp.where(causal, diff, -1e30))
                gs.append((cb * decay).astype(BF16))
            x_pair = xdt[:, pair * 2 * SSM_HEAD_DIM:(pair + 1) * 2 * SSM_HEAD_DIM]
            block_diag = jnp.concatenate(
                [jnp.where(first_head, x_pair, 0.0).astype(BF16),
                 jnp.where(first_head, 0.0, x_pair).astype(BF16)], axis=0)
            y_parts.append(_dot(jnp.concatenate(gs, axis=1), block_diag))
        y_diag = jnp.concatenate(y_parts, axis=1)

        prev = state_ref[:, ch]
        y_off = _dot(cm, prev.astype(BF16)) * out_decay_full[:, ch]
        new_states = _dot(bm_t, (xdt * state_decay_full[:, ch]).astype(BF16))
        state_ref[:, ch] = prev * out_decay_full[l - 1:l, ch] + new_states

        y = y_diag + y_off + dskip_ref[:, ch] * xs
        y_ref[:, ch] = y * _silu(zx_ref[:, ch].astype(F32))

    ext_ref[0:CONV_PAD_ROWS, :] = ext_ref[l:l + CONV_PAD_ROWS, :]
    o_ref[...] = _rms(y_ref[...], normw_ref[...]).astype(o_ref.dtype)


def _ssd(zx, dt_raw, conv_w, conv_b, dt_bias, a_log, d_skip, norm_w, bsz, d_inner, side_a, side_w, side_g,
         side_g_col0):
    side_tm = SSD_SIDE_ROW_CHUNKS * SSM_CHUNK
    side_k, side_n = side_w.shape
    side_tn = side_n // SSD_SIDE_ROW_CHUNKS
    side_g_off = side_g_col0 // side_tn

    def side_tile(b, c):
        step = b * nc + c
        return step // SSD_SIDE_ROW_CHUNKS, step % SSD_SIDE_ROW_CHUNKS

    t, width = zx.shape
    conv_dim = width - d_inner
    n_heads = d_inner // SSM_HEAD_DIM
    nc = t // bsz // SSM_CHUNK
    l = SSM_CHUNK

    def pad_heads(v):
        return jnp.zeros((1, LANES), F32).at[0, :n_heads].set(v.astype(F32))

    tri = jnp.asarray(np.tril(np.ones((l, l), np.float32)), BF16)
    expand = np.zeros((LANES, d_inner), np.float32)
    expand[np.arange(d_inner) // SSM_HEAD_DIM, np.arange(d_inner)] = 1.0
    expand = jnp.asarray(expand, BF16)
    d_skip_full = jnp.repeat(d_skip.astype(F32), SSM_HEAD_DIM).reshape(1, d_inner)

    def whole(shape):
        return pl.BlockSpec(shape, lambda b, c: (0, 0))

    return pl.pallas_call(
        functools.partial(_ssd_kernel, d_inner=d_inner),
        grid=(bsz, nc),
        in_specs=[
            pl.BlockSpec((l, width), lambda b, c: (b * nc + c, 0)),
            pl.BlockSpec((l, LANES), lambda b, c: (b * nc + c, 0)),
            whole((SSM_CONV, conv_dim)), whole((1, conv_dim)), whole((1, LANES)), whole((1, LANES)),
            whole((1, d_inner)), whole((1, d_inner)), whole((l, l)), whole((LANES, d_inner)),
            pl.BlockSpec((side_tm, side_k), lambda b, c: (side_tile(b, c)[0], 0)),
            pl.BlockSpec((side_k, side_tn), lambda b, c: (0, side_tile(b, c)[1])),
            pl.BlockSpec((side_tm, side_tn), lambda b, c: (side_tile(b, c)[0], side_tile(b, c)[1] + side_g_off)),
        ],
        out_specs=[pl.BlockSpec((l, d_inner), lambda b, c: (b * nc + c, 0)),
                   pl.BlockSpec((side_tm, side_tn), side_tile)],
        out_shape=[jax.ShapeDtypeStruct((t, d_inner), BF16), jax.ShapeDtypeStruct((t, side_n), BF16)],
        scratch_shapes=[
            pltpu.VMEM((CONV_PAD_ROWS + l, conv_dim), F32),
            pltpu.VMEM((SSM_D_STATE, d_inner), F32),
            pltpu.VMEM((l, d_inner), F32),
        ],
        compiler_params=_params(("arbitrary", "arbitrary"), VMEM_LIMIT_LARGE),
        name="ssd_mixer",
    )(zx, dt_raw, conv_w.astype(F32), conv_b.astype(F32).reshape(1, conv_dim), pad_heads(dt_bias),
      pad_heads(a_log), d_skip_full, norm_w.astype(F32).reshape(1, d_inner), tri, expand,
      side_a, side_w, side_g)


def _sb_kernel(q_ref, k_ref, v_ref, suffix_ref, *refs, scale, n_riders):
    rider_in, o_ref, rider_out = refs[:n_riders], refs[n_riders], refs[n_riders + 1:]
    _run_riders(rider_in, rider_out)

    blk = SB_BLOCK
    nq = q_ref.shape[0] // blk
    suffix01 = suffix_ref[...]
    row = jax.lax.broadcasted_iota(jnp.int32, (blk, blk), 0)
    col = jax.lax.broadcasted_iota(jnp.int32, (blk, blk), 1)
    strictly_causal = col < row

    def blocks(first, count=1):
        start = first * blk
        if not isinstance(start, int):
            start = pl.multiple_of(start, blk)
        return pl.ds(start, count * blk)

    def visit(q, j, carry, acc, mask):
        ks = blocks(j)
        z = jax.lax.dot_general(q, k_ref[ks, :], (((1,), (1,)), ((), ())),
                                preferred_element_type=F32) * scale
        sp = _softplus_abs(z)
        log_beta = z - sp
        log_1mb = -sp
        if mask is not None:
            log_1mb = jnp.where(mask, log_1mb, 0.0)
        hi, lo = _split_bf16(log_1mb, 2)
        sums = _dot(jnp.concatenate([hi, lo], axis=1), suffix01)
        w = jnp.exp(log_beta + sums[:, :blk] + carry)
        if mask is not None:
            w = jnp.where(mask, w, 0.0)
        acc = acc + _dot(w.astype(BF16), v_ref[ks, :])
        return carry + sums[:, blk:], acc

    def q_slice(i):
        return blocks(i)

    def finish(i, q, j, carry, acc):
        def cond(state):
            j, carry, _ = state
            return jnp.logical_and(j >= 0, jnp.max(carry) > SB_ZERO_LOG)

        def body(state):
            j, carry, acc = state
            carry, acc = visit(q, j, carry, acc, None)
            return j - 1, carry, acc

        _, _, acc = jax.lax.while_loop(cond, body, (jnp.int32(j), carry, acc))
        o_ref[q_slice(i), :] = acc.astype(o_ref.dtype)

    def single(i):
        q = q_ref[q_slice(i), :]
        zeros = jnp.zeros((blk, blk), F32)
        carry, acc = visit(q, i, zeros, zeros, strictly_causal)
        finish(i, q, i - 1, carry, acc)

    last = (SB_WINDOW - 1) * blk

    def window_scores(q, ks):
        z = jax.lax.dot_general(q, k_ref[ks, :], (((1,), (1,)), ((), ())),
                                preferred_element_type=F32) * scale
        sp = _softplus_abs(z)
        log_1mb = jnp.concatenate([-sp[:, :last], jnp.where(strictly_causal, -sp[:, last:], 0.0)], axis=1)
        hi, lo = _split_bf16(log_1mb, 2)
        stacked = jnp.concatenate(
            [jnp.concatenate([hi[:, b * blk:(b + 1) * blk], lo[:, b * blk:(b + 1) * blk]], axis=1)
             for b in range(SB_WINDOW)], axis=0)
        return z - sp, stacked

    def window_weights(log_beta, sums):
        later = jnp.zeros((blk, blk), F32)
        excl = [None] * SB_WINDOW
        for b in reversed(range(SB_WINDOW)):
            rows = slice(b * blk, (b + 1) * blk)
            excl[b] = sums[rows, :blk] + later
            later = later + sums[rows, blk:]
        w = jnp.exp(log_beta + jnp.concatenate(excl, axis=1))
        w = jnp.concatenate([w[:, :last], jnp.where(strictly_causal, w[:, last:], 0.0)], axis=1)
        return w.astype(BF16), later

    def group(first, size):
        idx = [first + g for g in range(size)]
        qs = [q_ref[q_slice(i), :] for i in idx]
        kss = [blocks(i - (SB_WINDOW - 1), SB_WINDOW) for i in idx]
        scored = [window_scores(q, ks) for q, ks in zip(qs, kss)]
        sums = [_dot(stacked, suffix01) for _, stacked in scored]
        weighted = [window_weights(log_beta, s) for (log_beta, _), s in zip(scored, sums)]
        accs = [_dot(w, v_ref[ks, :]) for (w, _), ks in zip(weighted, kss)]
        for i, acc in zip(idx, accs):
            o_ref[q_slice(i), :] = acc.astype(o_ref.dtype)
        carries = [carry for _, carry in weighted]

        @pl.when(jnp.max(functools.reduce(jnp.maximum, carries)) > SB_ZERO_LOG)
        def _():
            for i, q, carry, acc in zip(idx, qs, carries, accs):
                finish(i, q, i - SB_WINDOW, carry, acc)

    def full_group(ig, _):
        group(n_single + SB_GROUP * ig, SB_GROUP)
        return 0

    n_single = min(SB_WINDOW - 1, nq)
    n_groups, n_left = divmod(nq - n_single, SB_GROUP)
    for i in range(n_single):
        single(i)
    jax.lax.fori_loop(0, n_groups, full_group, 0)
    if n_left:
        group(nq - n_left, n_left)


def _stick_breaking(qkv, bsz, n_heads, riders=()):
    r_in, r_out, r_shapes = _rider_specs(riders, bsz * n_heads, lambda b, h: b * n_heads + h)
    t = qkv.shape[0]
    s = t // bsz
    dh = SB_HEAD_DIM
    blk = SB_BLOCK
    j_idx = np.arange(blk)[:, None]
    s_idx = np.arange(blk)[None, :]
    half = np.concatenate([(j_idx > s_idx).astype(np.float32), np.ones((blk, blk), np.float32)], axis=1)
    suffix01 = jnp.asarray(np.concatenate([half, half], axis=0), BF16)

    def head_spec(which):
        return pl.BlockSpec((s, dh), functools.partial(lambda b, h, which: (b, which * n_heads + h), which=which))

    out, *rider_out = pl.pallas_call(
        functools.partial(_sb_kernel, scale=float(1.0 / np.sqrt(dh).astype(np.float32)),
                          n_riders=len(riders)),
        grid=(bsz, n_heads),
        in_specs=[head_spec(0), head_spec(1), head_spec(2),
                  pl.BlockSpec((2 * blk, 2 * blk), lambda b, h: (0, 0))] + r_in,
        out_specs=[pl.BlockSpec((s, dh), lambda b, h: (b, h))] + r_out,
        out_shape=[jax.ShapeDtypeStruct((t, n_heads * dh), BF16)] + r_shapes,
        compiler_params=_params(("arbitrary", "arbitrary")),
        name="stick_breaking",
    )(qkv, qkv, qkv, suffix01, *[r.w for r in riders])
    return out, rider_out


def _layer(h, p, ffn1_pre_w, ffn1_w1, ffn1_w3, ffn1_w2, ffn1_post_w, mix_pre_w, w_in, conv_w, conv_b,
           dt_bias, a_log, d_skip, ssm_norm_w, w_ssm_out, w_sb_out, w_out, mix_post_w, ffn2_pre_w,
           ffn2_w1, ffn2_w3, ffn2_w2, ffn2_post_w, ple_w_gate, ple_w_proj, ple_norm_w, bsz):
    t, d = h.shape
    d_inner = w_ssm_out.shape[0]
    sb_width = w_sb_out.shape[0]
    n_ssm_heads = d_inner // SSM_HEAD_DIM
    conv_dim = conv_w.shape[1]
    bf = lambda w: w.astype(BF16)

    off_xbc = d_inner
    off_dt = off_xbc + conv_dim
    off_q = off_dt + n_ssm_heads
    off_gate = off_q + 3 * sb_width
    w_in_t = w_in.T
    (xn,) = _rowwise(_norm_fn, [h], [ffn1_pre_w], [BF16], "ffn1_pre_norm")
    f, (w_zx_t, w_qkv_t, w_gates_t) = _ffn(
        xn, bf(ffn1_w1), bf(ffn1_w3), bf(ffn1_w2), "ffn1",
        riders=[_Rider(w_in_t, 0, off_dt), _Rider(w_in_t, off_q, off_gate - off_q), _Rider(w_in_t, off_gate)])
    h, u = _rowwise(functools.partial(_residual_norm_fn, scale=0.5), [h, f], [ffn1_post_w, mix_pre_w],
                    [F32, BF16], "ffn1_residual_mix_norm")

    zx, _ = _mm([(u, w_zx_t)], BF16, "proj_z_xbc", _ep_identity, w_transposed=True)
    w_dt_t = jnp.zeros((LANES, d), BF16).at[:n_ssm_heads].set(bf(w_in_t[off_dt:off_q]))
    dt_raw, _ = _mm([(u, w_dt_t)], F32, "proj_dt", _ep_identity, w_transposed=True)
    qkv, (w_ssm_out_bf, w_sb_out_bf, w_out_bf, ple_w_gate_bf) = _mm(
        [(u, w_qkv_t)], BF16, "proj_qkv", _ep_identity, w_transposed=True,
        riders=[_Rider(w_ssm_out), _Rider(w_sb_out), _Rider(w_out), _Rider(ple_w_gate)])
    gates, _ = _mm([(u, w_gates_t)], BF16, "proj_gates", _ep_sigmoid, w_transposed=True)

    y_sb, ffn2_bf16 = _stick_breaking(qkv, bsz, sb_width // SB_HEAD_DIM,
                                      riders=[_Rider(ffn2_w1), _Rider(ffn2_w3), _Rider(ffn2_w2)])
    y_ssm, sb_branch = _ssd(zx, dt_raw, conv_w, conv_b, dt_bias, a_log, d_skip, ssm_norm_w, bsz, d_inner,
                            side_a=y_sb, side_w=w_sb_out_bf, side_g=gates, side_g_col0=d)
    merged, _ = _mm([(y_ssm, w_ssm_out_bf)], BF16, "ssm_out_merge", _ep_gated_add,
                    extras=[(gates, 0), (sb_branch, 0)])
    mix, _ = _mm([(merged, w_out_bf)], BF16, "mix_out", _ep_identity)
    h, xn = _rowwise(functools.partial(_residual_norm_fn, scale=1.0), [h, mix], [mix_post_w, ffn2_pre_w],
                     [F32, BF16], "mix_residual_ffn2_norm")

    f, _ = _ffn(xn, *ffn2_bf16, "ffn2")
    h, h_bf = _rowwise(functools.partial(_residual_cast_fn, scale=0.5), [h, f], [ffn2_post_w],
                       [F32, BF16], "ffn2_residual")

    ple, _ = _mm([(bf(p), bf(ple_w_proj)), (h_bf, ple_w_gate_bf)], BF16, "ple_gated", _ep_ple)
    (h,) = _rowwise(functools.partial(_residual_fn, scale=1.0), [h, ple], [ple_norm_w], [F32], "ple_residual")
    return h


def kernel(x, p, ffn1_pre_w, ffn1_w1, ffn1_w3, ffn1_w2, ffn1_post_w, mix_pre_w, w_in, conv_w, conv_b, dt_bias, a_log, d_skip, ssm_norm_w, w_ssm_out, w_sb_out, w_out, mix_post_w, ffn2_pre_w, ffn2_w1, ffn2_w3, ffn2_w2, ffn2_post_w, ple_w_gate, ple_w_proj, ple_norm_w):
    bsz, seqlen, d = x.shape
    depth = p.shape[0]
    h = x.reshape(bsz * seqlen, d)
    per_layer = (ffn1_pre_w, ffn1_w1, ffn1_w3, ffn1_w2, ffn1_post_w, mix_pre_w, w_in, conv_w, conv_b,
                 dt_bias, a_log, d_skip, ssm_norm_w, w_ssm_out, w_sb_out, w_out, mix_post_w, ffn2_pre_w,
                 ffn2_w1, ffn2_w3, ffn2_w2, ffn2_post_w, ple_w_gate, ple_w_proj, ple_norm_w)
    for i in range(depth):
        h = _layer(h, p[i].reshape(bsz * seqlen, -1), *[w[i] for w in per_layer], bsz=bsz)
    return h.reshape(bsz, seqlen, d)
```

```python
import functools
from typing import NamedTuple

import jax
import jax.numpy as jnp
import numpy as np
from jax.experimental import pallas as pl
from jax.experimental.pallas import tpu as pltpu

F32 = jnp.float32
BF16 = jnp.bfloat16

NORM_EPS = 1e-6
SSM_HEAD_DIM = 64
SSM_GROUPS = 8
SSM_D_STATE = 128
SSM_CONV = 4
SSM_CHUNK = 128
SSD_SIDE_ROW_CHUNKS = 4
SB_HEAD_DIM = 128
SB_BLOCK = 128
SB_WINDOW = 3
SB_GROUP = 8
LANES = 128
MXU_COLS = 256
BF16_SUBLANES = 16
CONV_PAD_ROWS = 8
VMEM_LIMIT = 56 * 1024 * 1024
VMEM_LIMIT_LARGE = 60 * 1024 * 1024
SB_ZERO_LOG = -105.0


def _params(semantics, vmem_limit=VMEM_LIMIT):
    return pltpu.CompilerParams(dimension_semantics=semantics, vmem_limit_bytes=vmem_limit)


def _split_bf16(x, parts):
    out = []
    rem = x
    for _ in range(parts):
        piece = rem.astype(BF16)
        out.append(piece)
        rem = rem - piece.astype(F32)
    return out


def _dot(a, b):
    return jnp.dot(a, b, preferred_element_type=F32)


def _dot_01(x, m01, parts, m01_on_left=False):
    acc = None
    for piece in _split_bf16(x, parts):
        term = _dot(m01, piece) if m01_on_left else _dot(piece, m01)
        acc = term if acc is None else acc + term
    return acc


def _softplus(x):
    return jnp.maximum(x, 0.0) + jnp.log1p(jnp.exp(-jnp.abs(x)))


def _softplus_abs(x):
    return jnp.maximum(x, 0.0) + jnp.log(1.0 + jnp.exp(-jnp.abs(x)))


def _silu(x):
    return x * jax.nn.sigmoid(x)


def _rms(x, w):
    x = x.astype(F32)
    return x * jax.lax.rsqrt(jnp.mean(x * x, axis=-1, keepdims=True) + NORM_EPS) * w


class _Rider(NamedTuple):
    w: jax.Array
    row0: int = 0
    rows: int | None = None

    def n_rows(self):
        return self.w.shape[0] - self.row0 if self.rows is None else self.rows


def _cast_row_block(row0, rows, steps):
    block = BF16_SUBLANES
    while rows % block or row0 % block or rows // block > steps:
        block += BF16_SUBLANES
        assert block <= rows, (row0, rows, steps)
    return block


def _rider_specs(riders, n_steps, step_of):
    in_specs, out_specs, out_shapes = [], [], []
    for rider in riders:
        rows, cols = rider.n_rows(), rider.w.shape[1]
        block = _cast_row_block(rider.row0, rows, n_steps)

        def index(*grid_idx, first=0, last=rows // block - 1):
            return first + jnp.minimum(step_of(*grid_idx), last), 0

        in_specs.append(pl.BlockSpec((block, cols), functools.partial(index, first=rider.row0 // block)))
        out_specs.append(pl.BlockSpec((block, cols), index))
        out_shapes.append(jax.ShapeDtypeStruct((rows, cols), BF16))
    return in_specs, out_specs, out_shapes


def _run_riders(in_refs, out_refs):
    for w_ref, wbf_ref in zip(in_refs, out_refs, strict=True):
        wbf_ref[...] = w_ref[...].astype(BF16)


def _rowwise_kernel(*refs, fn, n_rows, n_vecs):
    rows = [r[...] for r in refs[:n_rows]]
    vecs = [r[...] for r in refs[n_rows:n_rows + n_vecs]]
    outs = fn(rows, vecs)
    for o_ref, o in zip(refs[n_rows + n_vecs:], outs, strict=True):
        o_ref[...] = o.astype(o_ref.dtype)


def _rowwise(fn, rows, vecs, out_dtypes, name, tm=256):
    t, d = rows[0].shape
    tm = min(tm, t)
    row_spec = pl.BlockSpec((tm, d), lambda i: (i, 0))
    vec_spec = pl.BlockSpec((1, d), lambda i: (0, 0))
    return pl.pallas_call(
        functools.partial(_rowwise_kernel, fn=fn, n_rows=len(rows), n_vecs=len(vecs)),
        grid=(t // tm,),
        in_specs=[row_spec] * len(rows) + [vec_spec] * len(vecs),
        out_specs=[row_spec] * len(out_dtypes),
        out_shape=[jax.ShapeDtypeStruct((t, d), dt) for dt in out_dtypes],
        compiler_params=_params(("parallel",)),
        name=name,
    )(*rows, *[v.reshape(1, d) for v in vecs])


def _norm_fn(rows, vecs):
    return (_rms(rows[0], vecs[0]),)


def _residual_norm_fn(rows, vecs, *, scale):
    h = rows[0] + scale * _rms(rows[1], vecs[0])
    return h, _rms(h, vecs[1])


def _residual_cast_fn(rows, vecs, *, scale):
    h = rows[0] + scale * _rms(rows[1], vecs[0])
    return h, h


def _residual_fn(rows, vecs, *, scale):
    return (rows[0] + scale * _rms(rows[1], vecs[0]),)


def _ffn_kernel(xn_ref, w1_ref, w3_ref, w2_ref, *refs, n_riders):
    rider_in, o_ref, rider_out, acc_ref = refs[:n_riders], refs[n_riders], refs[n_riders + 1:-1], refs[-1]

    @pl.when(pl.program_id(1) == 0)
    def _():
        acc_ref[...] = jnp.zeros_like(acc_ref)

    _run_riders(rider_in, rider_out)
    x = xn_ref[...]
    h1 = _dot(x, w1_ref[...])
    h3 = _dot(x, w3_ref[...])
    g = (_silu(h1) * h3).astype(BF16)
    acc_ref[...] += _dot(g, w2_ref[...])

    @pl.when(pl.program_id(1) == pl.num_programs(1) - 1)
    def _():
        o_ref[...] = acc_ref[...].astype(o_ref.dtype)


def _ffn(xn, w1, w3, w2, name, riders=(), tm=1024, tf=256):
    t, d = xn.shape
    f = w1.shape[1]
    tm = min(tm, t)
    grid = (t // tm, f // tf)
    r_in, r_out, r_shapes = _rider_specs(riders, grid[0] * grid[1], lambda i, j: i * grid[1] + j)
    out, *rider_out = pl.pallas_call(
        functools.partial(_ffn_kernel, n_riders=len(riders)),
        grid=grid,
        in_specs=[
            pl.BlockSpec((tm, d), lambda i, j: (i, 0)),
            pl.BlockSpec((d, tf), lambda i, j: (0, j)),
            pl.BlockSpec((d, tf), lambda i, j: (0, j)),
            pl.BlockSpec((tf, d), lambda i, j: (j, 0)),
        ] + r_in,
        out_specs=[pl.BlockSpec((tm, d), lambda i, j: (i, 0), pipeline_mode=pl.Buffered(1))] + r_out,
        out_shape=[jax.ShapeDtypeStruct((t, d), BF16)] + r_shapes,
        scratch_shapes=[pltpu.VMEM((tm, d), F32)],
        compiler_params=_params(("arbitrary", "arbitrary"), VMEM_LIMIT_LARGE),
        name=name,
    )(xn, w1, w3, w2, *[r.w for r in riders])
    return out, rider_out


def _mm_kernel(*refs, n_pairs, n_extras, n_riders, epilogue, w_transposed):
    a_refs = refs[:n_pairs]
    w_refs = refs[n_pairs:2 * n_pairs]
    e_refs = refs[2 * n_pairs:2 * n_pairs + n_extras]
    rest = refs[2 * n_pairs + n_extras:]
    rider_in, o_ref, rider_out = rest[:n_riders], rest[n_riders], rest[n_riders + 1:]
    _run_riders(rider_in, rider_out)
    contract = (((1,), (1,)), ((), ())) if w_transposed else (((1,), (0,)), ((), ()))
    accs = [jax.lax.dot_general(a[...], w[...], contract, preferred_element_type=F32)
            for a, w in zip(a_refs, w_refs, strict=True)]
    o_ref[...] = epilogue(accs, [e[...] for e in e_refs]).astype(o_ref.dtype)


def _mm(pairs, out_dtype, name, epilogue, extras=(), riders=(), w_transposed=False, w_row0=0, n=None,
        tm=1024, tn=1024):
    t = pairs[0][0].shape[0]
    if n is None:
        n = pairs[0][1].shape[0 if w_transposed else 1] - w_row0
    tn = min(tn, n)
    tm = min(tm, t)
    assert n % tn == 0 and w_row0 % tn == 0 and (w_transposed or not w_row0)
    grid = (t // tm, n // tn)
    a_specs = [pl.BlockSpec((tm, a.shape[1]), lambda i, j: (i, 0)) for a, _ in pairs]
    if w_transposed:
        w_specs = [pl.BlockSpec((tn, w.shape[1]), lambda i, j: (j + w_row0 // tn, 0)) for _, w in pairs]
    else:
        w_specs = [pl.BlockSpec((w.shape[0], tn), lambda i, j: (0, j)) for _, w in pairs]
    e_specs = [pl.BlockSpec((tm, tn), functools.partial(lambda i, j, off: (i, j + off), off=off))
               for _, off in extras]
    r_in, r_out, r_shapes = _rider_specs(riders, grid[0] * grid[1], lambda i, j: i * grid[1] + j)
    out, *rider_out = pl.pallas_call(
        functools.partial(_mm_kernel, n_pairs=len(pairs), n_extras=len(extras), n_riders=len(riders),
                          epilogue=epilogue, w_transposed=w_transposed),
        grid=grid,
        in_specs=a_specs + w_specs + e_specs + r_in,
        out_specs=[pl.BlockSpec((tm, tn), lambda i, j: (i, j))] + r_out,
        out_shape=[jax.ShapeDtypeStruct((t, n), out_dtype)] + r_shapes,
        compiler_params=_params(("arbitrary", "arbitrary")),
        name=name,
    )(*[a for a, _ in pairs], *[w for _, w in pairs], *[e for e, _ in extras], *[r.w for r in riders])
    return out, rider_out


def _ep_identity(accs, extras):
    return accs[0]


def _ep_gated_add(accs, extras):
    return extras[0].astype(F32) * accs[0] + extras[1].astype(F32)


def _ep_sigmoid(accs, extras):
    return jax.nn.sigmoid(accs[0])


def _ep_ple(accs, extras):
    return accs[0] * jax.nn.sigmoid(accs[1])


def _ssd_kernel(z_ref, xbc_ref, dt_ref, convw_ref, convb_ref, dtb_ref, alog_ref, dskip_ref, normw_ref,
                tri_ref, expand_ref, side_a_ref, side_w_ref, side_g_ref, o_ref, side_o_ref,
                ext_ref, state_ref, y_ref, *, d_inner):
    l = SSM_CHUNK
    n = SSM_D_STATE
    gw = d_inner // SSM_GROUPS
    heads_per_group = gw // SSM_HEAD_DIM
    off_b = d_inner
    off_c = d_inner + SSM_GROUPS * n

    @pl.when(pl.program_id(1) == 0)
    def _():
        state_ref[...] = jnp.zeros_like(state_ref)
        ext_ref[0:CONV_PAD_ROWS, :] = jnp.zeros((CONV_PAD_ROWS, ext_ref.shape[1]), F32)

    ext_ref[CONV_PAD_ROWS:CONV_PAD_ROWS + l, :] = xbc_ref[...].astype(F32)

    def conv_silu(start, width):
        cols = slice(start, start + width)
        acc = convb_ref[:, cols]
        for k in range(SSM_CONV):
            row0 = CONV_PAD_ROWS - (SSM_CONV - 1) + k
            acc = acc + convw_ref[k:k + 1, cols] * ext_ref[row0:row0 + l, cols]
        return _silu(acc)

    dt = _softplus(dt_ref[...] + dtb_ref[...])
    da = dt * (-jnp.exp(alog_ref[...]))
    a_cum = _dot_01(da, tri_ref[...], 3, m01_on_left=True)
    a_cum_t = a_cum.T
    out_decay = jnp.exp(a_cum)
    state_decay = jnp.exp(a_cum[l - 1:l, :] - a_cum)
    expand = expand_ref[...]
    dt_full = _dot_01(dt, expand, 2)
    out_decay_full = _dot_01(out_decay, expand, 2)
    state_decay_full = _dot_01(state_decay, expand, 2)

    row = jax.lax.broadcasted_iota(jnp.int32, (l, l), 0)
    col = jax.lax.broadcasted_iota(jnp.int32, (l, l), 1)
    causal = col <= row
    lane = jax.lax.broadcasted_iota(jnp.int32, (l, 2 * SSM_HEAD_DIM), 1)
    first_head = lane < SSM_HEAD_DIM

    side_slabs = side_o_ref.shape[1] // MXU_COLS

    for g in range(SSM_GROUPS):
        if g % (SSM_GROUPS // side_slabs) == 0:
            slab = g // (SSM_GROUPS // side_slabs)
            sc = slice(slab * MXU_COLS, (slab + 1) * MXU_COLS)
            side_o_ref[:, sc] = (side_g_ref[:, sc].astype(F32) * _dot(side_a_ref[...], side_w_ref[:, sc])
                                 ).astype(side_o_ref.dtype)
        ch = slice(g * gw, (g + 1) * gw)
        xs = conv_silu(g * gw, gw)
        bm = conv_silu(off_b + g * n, n)
        cm = conv_silu(off_c + g * n, n).astype(BF16)
        bm_t = bm.T.astype(BF16)
        xdt = xs * dt_full[:, ch]
        cb = _dot(cm, bm_t)

        y_parts = []
        for pair in range(heads_per_group // 2):
            gs = []
            for r in (2 * pair, 2 * pair + 1):
                h = g * heads_per_group + r
                diff = a_cum[:, h:h + 1] - a_cum_t[h:h + 1, :]
                decay = jnp.exp(jnp.where(causal, diff, -1e30))
                gs.append((cb * decay).astype(BF16))
            x_pair = xdt[:, pair * 2 * SSM_HEAD_DIM:(pair + 1) * 2 * SSM_HEAD_DIM]
            block_diag = jnp.concatenate(
                [jnp.where(first_head, x_pair, 0.0).astype(BF16),
                 jnp.where(first_head, 0.0, x_pair).astype(BF16)], axis=0)
            y_parts.append(_dot(jnp.concatenate(gs, axis=1), block_diag))
        y_diag = jnp.concatenate(y_parts, axis=1)

        prev = state_ref[:, ch]
        y_off = _dot(cm, prev.astype(BF16)) * out_decay_full[:, ch]
        new_states = _dot(bm_t, (xdt * state_decay_full[:, ch]).astype(BF16))
        state_ref[:, ch] = prev * out_decay_full[l - 1:l, ch] + new_states

        y = y_diag + y_off + dskip_ref[:, ch] * xs
        z = jnp.concatenate([z_ref[g * (gw // LANES) + m] for m in range(gw // LANES)], axis=1).astype(F32)
        y_ref[:, ch] = y * _silu(z)

    ext_ref[0:CONV_PAD_ROWS, :] = ext_ref[l:l + CONV_PAD_ROWS, :]
    o_ref[...] = _rms(y_ref[...], normw_ref[...]).astype(o_ref.dtype)


def _ssd(z, xbc, dt_raw, conv_w, conv_b, dt_bias, a_log, d_skip, norm_w, bsz, side_a, side_w, side_g,
         side_g_col0):
    side_tm = SSD_SIDE_ROW_CHUNKS * SSM_CHUNK
    side_k, side_n = side_w.shape
    side_tn = side_n // SSD_SIDE_ROW_CHUNKS
    side_g_off = side_g_col0 // side_tn

    def side_tile(b, c):
        step = b * nc + c
        return step // SSD_SIDE_ROW_CHUNKS, step % SSD_SIDE_ROW_CHUNKS

    t, conv_dim = xbc.shape
    d_inner = z.shape[0] * LANES
    n_heads = d_inner // SSM_HEAD_DIM
    nc = t // bsz // SSM_CHUNK
    l = SSM_CHUNK

    def pad_heads(v):
        return jnp.zeros((1, LANES), F32).at[0, :n_heads].set(v.astype(F32))

    tri = jnp.asarray(np.tril(np.ones((l, l), np.float32)), BF16)
    expand = np.zeros((LANES, d_inner), np.float32)
    expand[np.arange(d_inner) // SSM_HEAD_DIM, np.arange(d_inner)] = 1.0
    expand = jnp.asarray(expand, BF16)
    d_skip_full = jnp.repeat(d_skip.astype(F32), SSM_HEAD_DIM).reshape(1, d_inner)

    def whole(shape):
        return pl.BlockSpec(shape, lambda b, c: (0, 0))

    return pl.pallas_call(
        functools.partial(_ssd_kernel, d_inner=d_inner),
        grid=(bsz, nc),
        in_specs=[
            pl.BlockSpec((d_inner // LANES, l, LANES), lambda b, c: (0, b * nc + c, 0)),
            pl.BlockSpec((l, conv_dim), lambda b, c: (b * nc + c, 0)),
            pl.BlockSpec((l, LANES), lambda b, c: (b * nc + c, 0)),
            whole((SSM_CONV, conv_dim)), whole((1, conv_dim)), whole((1, LANES)), whole((1, LANES)),
            whole((1, d_inner)), whole((1, d_inner)), whole((l, l)), whole((LANES, d_inner)),
            pl.BlockSpec((side_tm, side_k), lambda b, c: (side_tile(b, c)[0], 0)),
            pl.BlockSpec((side_k, side_tn), lambda b, c: (0, side_tile(b, c)[1])),
            pl.BlockSpec((side_tm, side_tn), lambda b, c: (side_tile(b, c)[0], side_tile(b, c)[1] + side_g_off)),
        ],
        out_specs=[pl.BlockSpec((l, d_inner), lambda b, c: (b * nc + c, 0)),
                   pl.BlockSpec((side_tm, side_tn), side_tile)],
        out_shape=[jax.ShapeDtypeStruct((t, d_inner), BF16), jax.ShapeDtypeStruct((t, side_n), BF16)],
        scratch_shapes=[
            pltpu.VMEM((CONV_PAD_ROWS + l, conv_dim), F32),
            pltpu.VMEM((SSM_D_STATE, d_inner), F32),
            pltpu.VMEM((l, d_inner), F32),
        ],
        compiler_params=_params(("arbitrary", "arbitrary"), VMEM_LIMIT_LARGE),
        name="ssd_mixer",
    )(z, xbc, dt_raw, conv_w.astype(F32), conv_b.astype(F32).reshape(1, conv_dim), pad_heads(dt_bias),
      pad_heads(a_log), d_skip_full, norm_w.astype(F32).reshape(1, d_inner), tri, expand,
      side_a, side_w, side_g)


def _sb_kernel(q_ref, k_ref, v_ref, suffix_ref, side_a_ref, side_w_ref, *refs, scale, n_riders):
    rider_in, o_ref, side_o_ref, rider_out = refs[:n_riders], refs[n_riders], refs[n_riders + 1], refs[n_riders + 2:]
    _run_riders(rider_in, rider_out)
    side_half = side_a_ref.shape[0] // 2

    def side_piece(k):
        def aligned(start, unit):
            return start if isinstance(start, int) else pl.multiple_of(start, unit)

        rows = pl.ds(aligned((k % 2) * side_half, side_half), side_half)
        cols = pl.ds(aligned((k // 2) * MXU_COLS, MXU_COLS), MXU_COLS)
        piece = jax.lax.dot_general(side_a_ref[rows, :], side_w_ref[cols, :],
                                    (((1,), (1,)), ((), ())), preferred_element_type=F32)
        for m in range(MXU_COLS // LANES):
            side_o_ref[(MXU_COLS // LANES) * (k // 2) + m, rows, :] = (
                piece[:, m * LANES:(m + 1) * LANES].astype(side_o_ref.dtype))

    blk = SB_BLOCK
    nq = q_ref.shape[0] // blk
    suffix01 = suffix_ref[...]
    row = jax.lax.broadcasted_iota(jnp.int32, (blk, blk), 0)
    col = jax.lax.broadcasted_iota(jnp.int32, (blk, blk), 1)
    strictly_causal = col < row

    def blocks(first, count=1):
        start = first * blk
        if not isinstance(start, int):
            start = pl.multiple_of(start, blk)
        return pl.ds(start, count * blk)

    def visit(q, j, carry, acc, mask):
        ks = blocks(j)
        z = jax.lax.dot_general(q, k_ref[ks, :], (((1,), (1,)), ((), ())),
                                preferred_element_type=F32) * scale
        sp = _softplus_abs(z)
        log_beta = z - sp
        log_1mb = -sp
        if mask is not None:
            log_1mb = jnp.where(mask, log_1mb, 0.0)
        hi, lo = _split_bf16(log_1mb, 2)
        sums = _dot(jnp.concatenate([hi, lo], axis=1), suffix01)
        w = jnp.exp(log_beta + sums[:, :blk] + carry)
        if mask is not None:
            w = jnp.where(mask, w, 0.0)
        acc = acc + _dot(w.astype(BF16), v_ref[ks, :])
        return carry + sums[:, blk:], acc

    def q_slice(i):
        return blocks(i)

    def finish(i, q, j, carry, acc):
        def cond(state):
            j, carry, _ = state
            return jnp.logical_and(j >= 0, jnp.max(carry) > SB_ZERO_LOG)

        def body(state):
            j, carry, acc = state
            carry, acc = visit(q, j, carry, acc, None)
            return j - 1, carry, acc

        _, _, acc = jax.lax.while_loop(cond, body, (jnp.int32(j), carry, acc))
        o_ref[q_slice(i), :] = acc.astype(o_ref.dtype)

    def single(i):
        q = q_ref[q_slice(i), :]
        zeros = jnp.zeros((blk, blk), F32)
        carry, acc = visit(q, i, zeros, zeros, strictly_causal)
        finish(i, q, i - 1, carry, acc)

    last = (SB_WINDOW - 1) * blk

    def window_scores(q, ks):
        z = jax.lax.dot_general(q, k_ref[ks, :], (((1,), (1,)), ((), ())),
                                preferred_element_type=F32) * scale
        sp = _softplus_abs(z)
        log_1mb = jnp.concatenate([-sp[:, :last], jnp.where(strictly_causal, -sp[:, last:], 0.0)], axis=1)
        hi, lo = _split_bf16(log_1mb, 2)
        stacked = jnp.concatenate(
            [jnp.concatenate([hi[:, b * blk:(b + 1) * blk], lo[:, b * blk:(b + 1) * blk]], axis=1)
             for b in range(SB_WINDOW)], axis=0)
        return z - sp, stacked

    def window_weights(log_beta, sums):
        later = jnp.zeros((blk, blk), F32)
        excl = [None] * SB_WINDOW
        for b in reversed(range(SB_WINDOW)):
            rows = slice(b * blk, (b + 1) * blk)
            excl[b] = sums[rows, :blk] + later
            later = later + sums[rows, blk:]
        w = jnp.exp(log_beta + jnp.concatenate(excl, axis=1))
        w = jnp.concatenate([w[:, :last], jnp.where(strictly_causal, w[:, last:], 0.0)], axis=1)
        return w.astype(BF16), later

    def group(first, size, piece):
        side_piece(piece)
        idx = [first + g for g in range(size)]
        qs = [q_ref[q_slice(i), :] for i in idx]
        kss = [blocks(i - (SB_WINDOW - 1), SB_WINDOW) for i in idx]
        scored = [window_scores(q, ks) for q, ks in zip(qs, kss)]
        sums = [_dot(stacked, suffix01) for _, stacked in scored]
        weighted = [window_weights(log_beta, s) for (log_beta, _), s in zip(scored, sums)]
        accs = [_dot(w, v_ref[ks, :]) for (w, _), ks in zip(weighted, kss)]
        for i, acc in zip(idx, accs):
            o_ref[q_slice(i), :] = acc.astype(o_ref.dtype)
        carries = [carry for _, carry in weighted]

        @pl.when(jnp.max(functools.reduce(jnp.maximum, carries)) > SB_ZERO_LOG)
        def _():
            for i, q, carry, acc in zip(idx, qs, carries, accs):
                finish(i, q, i - SB_WINDOW, carry, acc)

    def full_group(ig, _):
        group(n_single + SB_GROUP * ig, SB_GROUP, ig)
        return 0

    n_single, n_groups, n_left = _sb_schedule(nq)
    n_pieces = n_groups + (1 if n_left else 0)
    assert side_o_ref.shape[0] == n_pieces and n_pieces % 2 == 0
    for i in range(n_single):
        single(i)
    jax.lax.fori_loop(0, n_groups, full_group, 0)
    if n_left:
        group(nq - n_left, n_left, n_groups)


def _sb_schedule(nq):
    n_single = min(SB_WINDOW - 1, nq)
    return (n_single, *divmod(nq - n_single, SB_GROUP))


def _stick_breaking(qkv, bsz, n_heads, side_a, side_w_t, side_n, riders=(), side_tm=1024):
    r_in, r_out, r_shapes = _rider_specs(riders, bsz * n_heads, lambda b, h: b * n_heads + h)
    t = qkv.shape[0]
    s = t // bsz
    _, n_groups, n_left = _sb_schedule(s // SB_BLOCK)
    slabs = n_groups + (1 if n_left else 0)
    side_tm = min(side_tm, t)
    side_k = side_a.shape[1]
    n_col_tiles = side_n // (slabs * LANES)
    assert side_n % (slabs * LANES) == 0 and (t // side_tm) * n_col_tiles == bsz * n_heads, (t, side_n, slabs)

    def side_tile(b, h):
        step = b * n_heads + h
        return step // n_col_tiles, step % n_col_tiles

    side_in = [pl.BlockSpec((side_tm, side_k), lambda b, h: (side_tile(b, h)[0], 0)),
               pl.BlockSpec((slabs * LANES, side_k), lambda b, h: (side_tile(b, h)[1], 0))]
    side_out = pl.BlockSpec((slabs, side_tm, LANES), lambda b, h: (side_tile(b, h)[1], side_tile(b, h)[0], 0))
    dh = SB_HEAD_DIM
    blk = SB_BLOCK
    j_idx = np.arange(blk)[:, None]
    s_idx = np.arange(blk)[None, :]
    half = np.concatenate([(j_idx > s_idx).astype(np.float32), np.ones((blk, blk), np.float32)], axis=1)
    suffix01 = jnp.asarray(np.concatenate([half, half], axis=0), BF16)

    def head_spec(which):
        return pl.BlockSpec((s, dh), functools.partial(lambda b, h, which: (b, which * n_heads + h), which=which))

    out, side, *rider_out = pl.pallas_call(
        functools.partial(_sb_kernel, scale=float(1.0 / np.sqrt(dh).astype(np.float32)),
                          n_riders=len(riders)),
        grid=(bsz, n_heads),
        in_specs=[head_spec(0), head_spec(1), head_spec(2),
                  pl.BlockSpec((2 * blk, 2 * blk), lambda b, h: (0, 0))] + side_in + r_in,
        out_specs=[pl.BlockSpec((s, dh), lambda b, h: (b, h)), side_out] + r_out,
        out_shape=[jax.ShapeDtypeStruct((t, n_heads * dh), BF16),
                   jax.ShapeDtypeStruct((side_n // LANES, t, LANES), BF16)] + r_shapes,
        compiler_params=_params(("arbitrary", "arbitrary"), VMEM_LIMIT_LARGE),
        name="stick_breaking",
    )(qkv, qkv, qkv, suffix01, side_a, side_w_t, *[r.w for r in riders])
    return out, side, rider_out


def _layer(h, p, ffn1_pre_w, ffn1_w1, ffn1_w3, ffn1_w2, ffn1_post_w, mix_pre_w, w_in, conv_w, conv_b,
           dt_bias, a_log, d_skip, ssm_norm_w, w_ssm_out, w_sb_out, w_out, mix_post_w, ffn2_pre_w,
           ffn2_w1, ffn2_w3, ffn2_w2, ffn2_post_w, ple_w_gate, ple_w_proj, ple_norm_w, bsz):
    t, d = h.shape
    d_inner = w_ssm_out.shape[0]
    sb_width = w_sb_out.shape[0]
    n_ssm_heads = d_inner // SSM_HEAD_DIM
    conv_dim = conv_w.shape[1]
    bf = lambda w: w.astype(BF16)

    off_xbc = d_inner
    off_dt = off_xbc + conv_dim
    off_q = off_dt + n_ssm_heads
    off_gate = off_q + 3 * sb_width
    w_in_t = w_in.T
    (xn,) = _rowwise(_norm_fn, [h], [ffn1_pre_w], [BF16], "ffn1_pre_norm")
    f, (w_zx_t, w_qkv_t, w_gates_t) = _ffn(
        xn, bf(ffn1_w1), bf(ffn1_w3), bf(ffn1_w2), "ffn1",
        riders=[_Rider(w_in_t, 0, off_dt), _Rider(w_in_t, off_q, off_gate - off_q), _Rider(w_in_t, off_gate)])
    h, u = _rowwise(functools.partial(_residual_norm_fn, scale=0.5), [h, f], [ffn1_post_w, mix_pre_w],
                    [F32, BF16], "ffn1_residual_mix_norm")

    xbc, (ffn2_w1_bf,) = _mm([(u, w_zx_t)], BF16, "proj_xbc", _ep_identity, w_transposed=True,
                             w_row0=off_xbc, n=conv_dim, riders=[_Rider(ffn2_w1)])
    w_dt_t = jnp.zeros((LANES, d), BF16).at[:n_ssm_heads].set(bf(w_in_t[off_dt:off_q]))
    dt_raw, _ = _mm([(u, w_dt_t)], F32, "proj_dt", _ep_identity, w_transposed=True)
    qkv, (w_ssm_out_bf, w_sb_out_bf, w_out_bf, ple_w_gate_bf, ffn2_w2_bf) = _mm(
        [(u, w_qkv_t)], BF16, "proj_qkv", _ep_identity, w_transposed=True,
        riders=[_Rider(w_ssm_out), _Rider(w_sb_out), _Rider(w_out), _Rider(ple_w_gate), _Rider(ffn2_w2)])
    gates, (ffn2_w3_bf,) = _mm([(u, w_gates_t)], BF16, "proj_gates", _ep_sigmoid, w_transposed=True,
                               riders=[_Rider(ffn2_w3)])

    y_sb, z, _ = _stick_breaking(qkv, bsz, sb_width // SB_HEAD_DIM, side_a=u, side_w_t=w_zx_t, side_n=d_inner)
    y_ssm, sb_branch = _ssd(z, xbc, dt_raw, conv_w, conv_b, dt_bias, a_log, d_skip, ssm_norm_w, bsz,
                            side_a=y_sb, side_w=w_sb_out_bf, side_g=gates, side_g_col0=d)
    merged, _ = _mm([(y_ssm, w_ssm_out_bf)], BF16, "ssm_out_merge", _ep_gated_add,
                    extras=[(gates, 0), (sb_branch, 0)])
    mix, _ = _mm([(merged, w_out_bf)], BF16, "mix_out", _ep_identity)
    h, xn = _rowwise(functools.partial(_residual_norm_fn, scale=1.0), [h, mix], [mix_post_w, ffn2_pre_w],
                     [F32, BF16], "mix_residual_ffn2_norm")

    f, _ = _ffn(xn, ffn2_w1_bf, ffn2_w3_bf, ffn2_w2_bf, "ffn2")
    h, h_bf = _rowwise(functools.partial(_residual_cast_fn, scale=0.5), [h, f], [ffn2_post_w],
                       [F32, BF16], "ffn2_residual")

    ple, _ = _mm([(bf(p), bf(ple_w_proj)), (h_bf, ple_w_gate_bf)], BF16, "ple_gated", _ep_ple)
    (h,) = _rowwise(functools.partial(_residual_fn, scale=1.0), [h, ple], [ple_norm_w], [F32], "ple_residual")
    return h


def kernel(x, p, ffn1_pre_w, ffn1_w1, ffn1_w3, ffn1_w2, ffn1_post_w, mix_pre_w, w_in, conv_w, conv_b, dt_bias, a_log, d_skip, ssm_norm_w, w_ssm_out, w_sb_out, w_out, mix_post_w, ffn2_pre_w, ffn2_w1, ffn2_w3, ffn2_w2, ffn2_post_w, ple_w_gate, ple_w_proj, ple_norm_w):
    bsz, seqlen, d = x.shape
    depth = p.shape[0]
    h = x.reshape(bsz * seqlen, d)
    per_layer = (ffn1_pre_w, ffn1_w1, ffn1_w3, ffn1_w2, ffn1_post_w, mix_pre_w, w_in, conv_w, conv_b,
                 dt_bias, a_log, d_skip, ssm_norm_w, w_ssm_out, w_sb_out, w_out, mix_post_w, ffn2_pre_w,
                 ffn2_w1, ffn2_w3, ffn2_w2, ffn2_post_w, ple_w_gate, ple_w_proj, ple_norm_w)
    for i in range(depth):
        h = _layer(h, p[i].reshape(bsz * seqlen, -1), *[w[i] for w in per_layer], bsz=bsz)
    return h.reshape(bsz, seqlen, d)
```

```python
import functools
from typing import NamedTuple

import jax
import jax.numpy as jnp
import numpy as np
from jax.experimental import pallas as pl
from jax.experimental.pallas import tpu as pltpu

F32 = jnp.float32
BF16 = jnp.bfloat16

NORM_EPS = 1e-6
SSM_HEAD_DIM = 64
SSM_GROUPS = 8
SSM_D_STATE = 128
SSM_CONV = 4
SSM_CHUNK = 128
SSD_SIDE_ROW_CHUNKS = 4
SB_HEAD_DIM = 128
SB_BLOCK = 128
SB_WINDOW = 3
SB_GROUP = 8
LANES = 128
MXU_COLS = 256
BF16_SUBLANES = 16
CONV_PAD_ROWS = 8
VMEM_LIMIT = 56 * 1024 * 1024
VMEM_LIMIT_LARGE = 60 * 1024 * 1024
SB_ZERO_LOG = -105.0


def _params(semantics, vmem_limit=VMEM_LIMIT):
    return pltpu.CompilerParams(dimension_semantics=semantics, vmem_limit_bytes=vmem_limit)


def _split_bf16(x, parts):
    out = []
    rem = x
    for _ in range(parts):
        piece = rem.astype(BF16)
        out.append(piece)
        rem = rem - piece.astype(F32)
    return out


def _dot(a, b):
    return jnp.dot(a, b, preferred_element_type=F32)


def _dot_01(x, m01, parts, m01_on_left=False):
    acc = None
    for piece in _split_bf16(x, parts):
        term = _dot(m01, piece) if m01_on_left else _dot(piece, m01)
        acc = term if acc is None else acc + term
    return acc


def _softplus(x):
    return jnp.maximum(x, 0.0) + jnp.log1p(jnp.exp(-jnp.abs(x)))


def _softplus_abs(x):
    return jnp.maximum(x, 0.0) + jnp.log(1.0 + jnp.exp(-jnp.abs(x)))


def _silu(x):
    return x * jax.nn.sigmoid(x)


def _rms(x, w):
    x = x.astype(F32)
    return x * jax.lax.rsqrt(jnp.mean(x * x, axis=-1, keepdims=True) + NORM_EPS) * w


class _Rider(NamedTuple):
    w: jax.Array
    row0: int = 0
    rows: int | None = None

    def n_rows(self):
        return self.w.shape[0] - self.row0 if self.rows is None else self.rows


def _cast_row_block(row0, rows, steps):
    block = BF16_SUBLANES
    while rows % block or row0 % block or rows // block > steps:
        block += BF16_SUBLANES
        assert block <= rows, (row0, rows, steps)
    return block


def _rider_specs(riders, n_steps, step_of):
    in_specs, out_specs, out_shapes = [], [], []
    for rider in riders:
        rows, cols = rider.n_rows(), rider.w.shape[1]
        block = _cast_row_block(rider.row0, rows, n_steps)

        def index(*grid_idx, first=0, last=rows // block - 1):
            return first + jnp.minimum(step_of(*grid_idx), last), 0

        in_specs.append(pl.BlockSpec((block, cols), functools.partial(index, first=rider.row0 // block)))
        out_specs.append(pl.BlockSpec((block, cols), index))
        out_shapes.append(jax.ShapeDtypeStruct((rows, cols), BF16))
    return in_specs, out_specs, out_shapes


def _run_riders(in_refs, out_refs):
    for w_ref, wbf_ref in zip(in_refs, out_refs, strict=True):
        wbf_ref[...] = w_ref[...].astype(BF16)


def _rowwise_kernel(*refs, fn, n_rows, n_vecs):
    rows = [r[...] for r in refs[:n_rows]]
    vecs = [r[...] for r in refs[n_rows:n_rows + n_vecs]]
    outs = fn(rows, vecs)
    for o_ref, o in zip(refs[n_rows + n_vecs:], outs, strict=True):
        o_ref[...] = o.astype(o_ref.dtype)


def _rowwise(fn, rows, vecs, out_dtypes, name, tm=256):
    t, d = rows[0].shape
    tm = min(tm, t)
    row_spec = pl.BlockSpec((tm, d), lambda i: (i, 0))
    vec_spec = pl.BlockSpec((1, d), lambda i: (0, 0))
    return pl.pallas_call(
        functools.partial(_rowwise_kernel, fn=fn, n_rows=len(rows), n_vecs=len(vecs)),
        grid=(t // tm,),
        in_specs=[row_spec] * len(rows) + [vec_spec] * len(vecs),
        out_specs=[row_spec] * len(out_dtypes),
        out_shape=[jax.ShapeDtypeStruct((t, d), dt) for dt in out_dtypes],
        compiler_params=_params(("parallel",)),
        name=name,
    )(*rows, *[v.reshape(1, d) for v in vecs])


def _norm_fn(rows, vecs):
    return (_rms(rows[0], vecs[0]),)


def _residual_norm_fn(rows, vecs, *, scale):
    h = rows[0] + scale * _rms(rows[1], vecs[0])
    return h, _rms(h, vecs[1])


def _residual_cast_fn(rows, vecs, *, scale):
    h = rows[0] + scale * _rms(rows[1], vecs[0])
    return h, h


def _residual_fn(rows, vecs, *, scale):
    return (rows[0] + scale * _rms(rows[1], vecs[0]),)


def _ffn_kernel(xn_ref, w1_ref, w3_ref, w2_ref, *refs, n_riders):
    rider_in, o_ref, rider_out, acc_ref = refs[:n_riders], refs[n_riders], refs[n_riders + 1:-1], refs[-1]

    @pl.when(pl.program_id(1) == 0)
    def _():
        acc_ref[...] = jnp.zeros_like(acc_ref)

    _run_riders(rider_in, rider_out)
    x = xn_ref[...]
    h1 = _dot(x, w1_ref[...])
    h3 = _dot(x, w3_ref[...])
    g = (_silu(h1) * h3).astype(BF16)
    acc_ref[...] += _dot(g, w2_ref[...])

    @pl.when(pl.program_id(1) == pl.num_programs(1) - 1)
    def _():
        o_ref[...] = acc_ref[...].astype(o_ref.dtype)


def _ffn(xn, w1, w3, w2, name, riders=(), tm=1024, tf=256):
    t, d = xn.shape
    f = w1.shape[1]
    tm = min(tm, t)
    grid = (t // tm, f // tf)
    r_in, r_out, r_shapes = _rider_specs(riders, grid[0] * grid[1], lambda i, j: i * grid[1] + j)
    out, *rider_out = pl.pallas_call(
        functools.partial(_ffn_kernel, n_riders=len(riders)),
        grid=grid,
        in_specs=[
            pl.BlockSpec((tm, d), lambda i, j: (i, 0)),
            pl.BlockSpec((d, tf), lambda i, j: (0, j)),
            pl.BlockSpec((d, tf), lambda i, j: (0, j)),
            pl.BlockSpec((tf, d), lambda i, j: (j, 0)),
        ] + r_in,
        out_specs=[pl.BlockSpec((tm, d), lambda i, j: (i, 0), pipeline_mode=pl.Buffered(1))] + r_out,
        out_shape=[jax.ShapeDtypeStruct((t, d), BF16)] + r_shapes,
        scratch_shapes=[pltpu.VMEM((tm, d), F32)],
        compiler_params=_params(("arbitrary", "arbitrary"), VMEM_LIMIT_LARGE),
        name=name,
    )(xn, w1, w3, w2, *[r.w for r in riders])
    return out, rider_out


def _mm_kernel(*refs, n_pairs, n_extras, n_riders, epilogue, w_transposed):
    a_refs = refs[:n_pairs]
    w_refs = refs[n_pairs:2 * n_pairs]
    e_refs = refs[2 * n_pairs:2 * n_pairs + n_extras]
    rest = refs[2 * n_pairs + n_extras:]
    rider_in, o_ref, rider_out = rest[:n_riders], rest[n_riders], rest[n_riders + 1:]
    _run_riders(rider_in, rider_out)
    contract = (((1,), (1,)), ((), ())) if w_transposed else (((1,), (0,)), ((), ()))
    accs = [jax.lax.dot_general(a[...], w[...], contract, preferred_element_type=F32)
            for a, w in zip(a_refs, w_refs, strict=True)]
    o_ref[...] = epilogue(accs, [e[...] for e in e_refs]).astype(o_ref.dtype)


def _mm(pairs, out_dtype, name, epilogue, extras=(), riders=(), w_transposed=False, w_row0=0, n=None,
        tm=1024, tn=1024, vmem_limit=VMEM_LIMIT):
    t = pairs[0][0].shape[0]
    if n is None:
        n = pairs[0][1].shape[0 if w_transposed else 1] - w_row0
    tn = min(tn, n)
    tm = min(tm, t)
    assert n % tn == 0 and w_row0 % tn == 0 and (w_transposed or not w_row0)
    grid = (t // tm, n // tn)
    a_specs = [pl.BlockSpec((tm, a.shape[1]), lambda i, j: (i, 0)) for a, _ in pairs]
    if w_transposed:
        w_specs = [pl.BlockSpec((tn, w.shape[1]), lambda i, j: (j + w_row0 // tn, 0)) for _, w in pairs]
    else:
        w_specs = [pl.BlockSpec((w.shape[0], tn), lambda i, j: (0, j)) for _, w in pairs]
    e_specs = [pl.BlockSpec((tm, tn), functools.partial(lambda i, j, off: (i, j + off), off=off))
               for _, off in extras]
    r_in, r_out, r_shapes = _rider_specs(riders, grid[0] * grid[1], lambda i, j: i * grid[1] + j)
    out, *rider_out = pl.pallas_call(
        functools.partial(_mm_kernel, n_pairs=len(pairs), n_extras=len(extras), n_riders=len(riders),
                          epilogue=epilogue, w_transposed=w_transposed),
        grid=grid,
        in_specs=a_specs + w_specs + e_specs + r_in,
        out_specs=[pl.BlockSpec((tm, tn), lambda i, j: (i, j))] + r_out,
        out_shape=[jax.ShapeDtypeStruct((t, n), out_dtype)] + r_shapes,
        compiler_params=_params(("arbitrary", "arbitrary"), vmem_limit),
        name=name,
    )(*[a for a, _ in pairs], *[w for _, w in pairs], *[e for e, _ in extras], *[r.w for r in riders])
    return out, rider_out


def _ep_identity(accs, extras):
    return accs[0]


def _ep_gated_add(accs, extras):
    return extras[0].astype(F32) * accs[0] + extras[1].astype(F32)


def _ep_sigmoid(accs, extras):
    return jax.nn.sigmoid(accs[0])


def _ep_ple(accs, extras):
    return accs[0] * jax.nn.sigmoid(accs[1])


def _ssd_kernel(z_ref, xbc_ref, dt_ref, convw_ref, convb_ref, dtb_ref, alog_ref, dskip_ref, normw_ref,
                tri_ref, expand_ref, side_a_ref, side_w_ref, side_g_ref, o_ref, side_o_ref,
                ext_ref, state_ref, y_ref, *, d_inner):
    l = SSM_CHUNK
    n = SSM_D_STATE
    gw = d_inner // SSM_GROUPS
    heads_per_group = gw // SSM_HEAD_DIM
    off_b = d_inner
    off_c = d_inner + SSM_GROUPS * n

    @pl.when(pl.program_id(1) == 0)
    def _():
        state_ref[...] = jnp.zeros_like(state_ref)
        ext_ref[0:CONV_PAD_ROWS, :] = jnp.zeros((CONV_PAD_ROWS, ext_ref.shape[1]), F32)

    ext_ref[CONV_PAD_ROWS:CONV_PAD_ROWS + l, :] = xbc_ref[...].astype(F32)

    def conv_silu(start, width):
        cols = slice(start, start + width)
        acc = convb_ref[:, cols]
        for k in range(SSM_CONV):
            row0 = CONV_PAD_ROWS - (SSM_CONV - 1) + k
            acc = acc + convw_ref[k:k + 1, cols] * ext_ref[row0:row0 + l, cols]
        return _silu(acc)

    dt = _softplus(dt_ref[...] + dtb_ref[...])
    da = dt * (-jnp.exp(alog_ref[...]))
    a_cum = _dot_01(da, tri_ref[...], 3, m01_on_left=True)
    a_cum_t = a_cum.T
    out_decay = jnp.exp(a_cum)
    state_decay = jnp.exp(a_cum[l - 1:l, :] - a_cum)
    pieces = [p for x in (dt, out_decay, state_decay) for p in _split_bf16(x, 2)]
    expanded = _dot(jnp.concatenate(pieces, axis=0), expand_ref[...])
    dt_full, out_decay_full, state_decay_full = (
        expanded[2 * k * l:(2 * k + 1) * l] + expanded[(2 * k + 1) * l:(2 * k + 2) * l] for k in range(3))

    row = jax.lax.broadcasted_iota(jnp.int32, (l, l), 0)
    col = jax.lax.broadcasted_iota(jnp.int32, (l, l), 1)
    causal = col <= row
    lane = jax.lax.broadcasted_iota(jnp.int32, (l, 2 * SSM_HEAD_DIM), 1)
    first_head = lane < SSM_HEAD_DIM

    side_slabs = side_o_ref.shape[1] // MXU_COLS

    for g in range(SSM_GROUPS):
        if g % (SSM_GROUPS // side_slabs) == 0:
            slab = g // (SSM_GROUPS // side_slabs)
            sc = slice(slab * MXU_COLS, (slab + 1) * MXU_COLS)
            side_o_ref[:, sc] = (side_g_ref[:, sc].astype(F32) * _dot(side_a_ref[...], side_w_ref[:, sc])
                                 ).astype(side_o_ref.dtype)
        ch = slice(g * gw, (g + 1) * gw)
        xs = conv_silu(g * gw, gw)
        bm = conv_silu(off_b + g * n, n)
        cm = conv_silu(off_c + g * n, n).astype(BF16)
        bm_t = bm.T.astype(BF16)
        xdt = xs * dt_full[:, ch]
        cb = _dot(cm, bm_t)

        y_parts = []
        for pair in range(heads_per_group // 2):
            gs = []
            for r in (2 * pair, 2 * pair + 1):
                h = g * heads_per_group + r
                diff = a_cum[:, h:h + 1] - a_cum_t[h:h + 1, :]
                decay = jnp.exp(jnp.where(causal, diff, -1e30))
                gs.append((cb * decay).astype(BF16))
            x_pair = xdt[:, pair * 2 * SSM_HEAD_DIM:(pair + 1) * 2 * SSM_HEAD_DIM]
            block_diag = jnp.concatenate(
                [jnp.where(first_head, x_pair, 0.0).astype(BF16),
                 jnp.where(first_head, 0.0, x_pair).astype(BF16)], axis=0)
            y_parts.append(_dot(jnp.concatenate(gs, axis=1), block_diag))
        y_diag = jnp.concatenate(y_parts, axis=1)

        prev = state_ref[:, ch]
        y_off = _dot(cm, prev.astype(BF16)) * out_decay_full[:, ch]
        new_states = _dot(bm_t, (xdt * state_decay_full[:, ch]).astype(BF16))
        state_ref[:, ch] = prev * out_decay_full[l - 1:l, ch] + new_states

        y = y_diag + y_off + dskip_ref[:, ch] * xs
        z = jnp.concatenate([z_ref[g * (gw // LANES) + m] for m in range(gw // LANES)], axis=1).astype(F32)
        y_ref[:, ch] = y * _silu(z)

    ext_ref[0:CONV_PAD_ROWS, :] = ext_ref[l:l + CONV_PAD_ROWS, :]
    o_ref[...] = _rms(y_ref[...], normw_ref[...]).astype(o_ref.dtype)


def _ssd(z, xbc, dt_raw, conv_w, conv_b, dt_bias, a_log, d_skip, norm_w, bsz, side_a, side_w, side_g,
         side_g_col0):
    side_tm = SSD_SIDE_ROW_CHUNKS * SSM_CHUNK
    side_k, side_n = side_w.shape
    side_tn = side_n // SSD_SIDE_ROW_CHUNKS
    side_g_off = side_g_col0 // side_tn

    def side_tile(b, c):
        step = b * nc + c
        return step // SSD_SIDE_ROW_CHUNKS, step % SSD_SIDE_ROW_CHUNKS

    t, conv_dim = xbc.shape
    d_inner = z.shape[0] * LANES
    n_heads = d_inner // SSM_HEAD_DIM
    nc = t // bsz // SSM_CHUNK
    l = SSM_CHUNK

    def pad_heads(v):
        return jnp.zeros((1, LANES), F32).at[0, :n_heads].set(v.astype(F32))

    tri = jnp.asarray(np.tril(np.ones((l, l), np.float32)), BF16)
    expand = np.zeros((LANES, d_inner), np.float32)
    expand[np.arange(d_inner) // SSM_HEAD_DIM, np.arange(d_inner)] = 1.0
    expand = jnp.asarray(expand, BF16)
    d_skip_full = jnp.repeat(d_skip.astype(F32), SSM_HEAD_DIM).reshape(1, d_inner)

    def whole(shape):
        return pl.BlockSpec(shape, lambda b, c: (0, 0))

    return pl.pallas_call(
        functools.partial(_ssd_kernel, d_inner=d_inner),
        grid=(bsz, nc),
        in_specs=[
            pl.BlockSpec((d_inner // LANES, l, LANES), lambda b, c: (0, b * nc + c, 0)),
            pl.BlockSpec((l, conv_dim), lambda b, c: (b * nc + c, 0)),
            pl.BlockSpec((l, LANES), lambda b, c: (b * nc + c, 0)),
            whole((SSM_CONV, conv_dim)), whole((1, conv_dim)), whole((1, LANES)), whole((1, LANES)),
            whole((1, d_inner)), whole((1, d_inner)), whole((l, l)), whole((LANES, d_inner)),
            pl.BlockSpec((side_tm, side_k), lambda b, c: (side_tile(b, c)[0], 0)),
            pl.BlockSpec((side_k, side_tn), lambda b, c: (0, side_tile(b, c)[1])),
            pl.BlockSpec((side_tm, side_tn), lambda b, c: (side_tile(b, c)[0], side_tile(b, c)[1] + side_g_off)),
        ],
        out_specs=[pl.BlockSpec((l, d_inner), lambda b, c: (b * nc + c, 0)),
                   pl.BlockSpec((side_tm, side_tn), side_tile)],
        out_shape=[jax.ShapeDtypeStruct((t, d_inner), BF16), jax.ShapeDtypeStruct((t, side_n), BF16)],
        scratch_shapes=[
            pltpu.VMEM((CONV_PAD_ROWS + l, conv_dim), F32),
            pltpu.VMEM((SSM_D_STATE, d_inner), F32),
            pltpu.VMEM((l, d_inner), F32),
        ],
        compiler_params=_params(("arbitrary", "arbitrary"), VMEM_LIMIT_LARGE),
        name="ssd_mixer",
    )(z, xbc, dt_raw, conv_w.astype(F32), conv_b.astype(F32).reshape(1, conv_dim), pad_heads(dt_bias),
      pad_heads(a_log), d_skip_full, norm_w.astype(F32).reshape(1, d_inner), tri, expand,
      side_a, side_w, side_g)


def _sb_kernel(q_ref, k_ref, v_ref, suffix_ref, side_a_ref, side_w_ref, *refs, scale, n_riders):
    rider_in, o_ref, side_o_ref, rider_out = refs[:n_riders], refs[n_riders], refs[n_riders + 1], refs[n_riders + 2:]
    _run_riders(rider_in, rider_out)
    side_half = side_a_ref.shape[0] // 2

    def side_piece(k):
        def aligned(start, unit):
            return start if isinstance(start, int) else pl.multiple_of(start, unit)

        rows = pl.ds(aligned((k % 2) * side_half, side_half), side_half)
        cols = pl.ds(aligned((k // 2) * MXU_COLS, MXU_COLS), MXU_COLS)
        piece = jax.lax.dot_general(side_a_ref[rows, :], side_w_ref[cols, :],
                                    (((1,), (1,)), ((), ())), preferred_element_type=F32)
        for m in range(MXU_COLS // LANES):
            side_o_ref[(MXU_COLS // LANES) * (k // 2) + m, rows, :] = (
                piece[:, m * LANES:(m + 1) * LANES].astype(side_o_ref.dtype))

    blk = SB_BLOCK
    nq = q_ref.shape[0] // blk
    suffix01 = suffix_ref[...]
    row = jax.lax.broadcasted_iota(jnp.int32, (blk, blk), 0)
    col = jax.lax.broadcasted_iota(jnp.int32, (blk, blk), 1)
    strictly_causal = col < row

    def blocks(first, count=1):
        start = first * blk
        if not isinstance(start, int):
            start = pl.multiple_of(start, blk)
        return pl.ds(start, count * blk)

    def visit(q, j, carry, acc, mask):
        ks = blocks(j)
        z = jax.lax.dot_general(q, k_ref[ks, :], (((1,), (1,)), ((), ())),
                                preferred_element_type=F32) * scale
        sp = _softplus_abs(z)
        log_beta = z - sp
        log_1mb = -sp
        if mask is not None:
            log_1mb = jnp.where(mask, log_1mb, 0.0)
        hi, lo = _split_bf16(log_1mb, 2)
        sums = _dot(jnp.concatenate([hi, lo], axis=1), suffix01)
        w = jnp.exp(log_beta + sums[:, :blk] + carry)
        if mask is not None:
            w = jnp.where(mask, w, 0.0)
        acc = acc + _dot(w.astype(BF16), v_ref[ks, :])
        return carry + sums[:, blk:], acc

    def q_slice(i):
        return blocks(i)

    def finish(i, q, j, carry, acc):
        def cond(state):
            j, carry, _ = state
            return jnp.logical_and(j >= 0, jnp.max(carry) > SB_ZERO_LOG)

        def body(state):
            j, carry, acc = state
            carry, acc = visit(q, j, carry, acc, None)
            return j - 1, carry, acc

        _, _, acc = jax.lax.while_loop(cond, body, (jnp.int32(j), carry, acc))
        o_ref[q_slice(i), :] = acc.astype(o_ref.dtype)

    def single(i):
        q = q_ref[q_slice(i), :]
        zeros = jnp.zeros((blk, blk), F32)
        carry, acc = visit(q, i, zeros, zeros, strictly_causal)
        for j in reversed(range(i)):
            carry, acc = visit(q, j, carry, acc, None)
        o_ref[q_slice(i), :] = acc.astype(o_ref.dtype)

    last = (SB_WINDOW - 1) * blk

    def window_scores(q, ks):
        z = jax.lax.dot_general(q, k_ref[ks, :], (((1,), (1,)), ((), ())),
                                preferred_element_type=F32) * scale
        sp = _softplus_abs(z)
        log_1mb = jnp.concatenate([-sp[:, :last], jnp.where(strictly_causal, -sp[:, last:], 0.0)], axis=1)
        hi, lo = _split_bf16(log_1mb, 2)
        stacked = jnp.concatenate(
            [jnp.concatenate([hi[:, b * blk:(b + 1) * blk], lo[:, b * blk:(b + 1) * blk]], axis=1)
             for b in range(SB_WINDOW)], axis=0)
        return z - sp, stacked

    def window_weights(log_beta, sums):
        later = jnp.zeros((blk, blk), F32)
        excl = [None] * SB_WINDOW
        for b in reversed(range(SB_WINDOW)):
            rows = slice(b * blk, (b + 1) * blk)
            excl[b] = sums[rows, :blk] + later
            later = later + sums[rows, blk:]
        w = jnp.exp(log_beta + jnp.concatenate(excl, axis=1))
        w = jnp.concatenate([w[:, :last], jnp.where(strictly_causal, w[:, last:], 0.0)], axis=1)
        return w.astype(BF16), later

    def group(first, size, piece):
        side_piece(piece)
        idx = [first + g for g in range(size)]
        qs = [q_ref[q_slice(i), :] for i in idx]
        kss = [blocks(i - (SB_WINDOW - 1), SB_WINDOW) for i in idx]
        scored = [window_scores(q, ks) for q, ks in zip(qs, kss)]
        sums = [_dot(stacked, suffix01) for _, stacked in scored]
        weighted = [window_weights(log_beta, s) for (log_beta, _), s in zip(scored, sums)]
        accs = [_dot(w, v_ref[ks, :]) for (w, _), ks in zip(weighted, kss)]
        for i, acc in zip(idx, accs):
            o_ref[q_slice(i), :] = acc.astype(o_ref.dtype)
        carries = [carry for _, carry in weighted]

        @pl.when(jnp.max(functools.reduce(jnp.maximum, carries)) > SB_ZERO_LOG)
        def _():
            for i, q, carry, acc in zip(idx, qs, carries, accs):
                finish(i, q, i - SB_WINDOW, carry, acc)

    def full_group(ig, _):
        group(n_single + SB_GROUP * ig, SB_GROUP, ig)
        return 0

    n_single, n_groups, n_left = _sb_schedule(nq)
    n_pieces = n_groups + (1 if n_left else 0)
    assert side_o_ref.shape[0] == n_pieces and n_pieces % 2 == 0
    for i in range(n_single):
        single(i)
    jax.lax.fori_loop(0, n_groups, full_group, 0)
    if n_left:
        group(nq - n_left, n_left, n_groups)


def _sb_schedule(nq):
    n_single = min(SB_WINDOW - 1, nq)
    return (n_single, *divmod(nq - n_single, SB_GROUP))


def _stick_breaking(qkv, bsz, n_heads, side_a, side_w_t, side_n, riders=(), side_tm=1024):
    r_in, r_out, r_shapes = _rider_specs(riders, bsz * n_heads, lambda b, h: b * n_heads + h)
    t = qkv.shape[0]
    s = t // bsz
    _, n_groups, n_left = _sb_schedule(s // SB_BLOCK)
    slabs = n_groups + (1 if n_left else 0)
    side_tm = min(side_tm, t)
    side_k = side_a.shape[1]
    n_col_tiles = side_n // (slabs * LANES)
    assert side_n % (slabs * LANES) == 0 and (t // side_tm) * n_col_tiles == bsz * n_heads, (t, side_n, slabs)

    def side_tile(b, h):
        step = b * n_heads + h
        return step // n_col_tiles, step % n_col_tiles

    side_in = [pl.BlockSpec((side_tm, side_k), lambda b, h: (side_tile(b, h)[0], 0)),
               pl.BlockSpec((slabs * LANES, side_k), lambda b, h: (side_tile(b, h)[1], 0))]
    side_out = pl.BlockSpec((slabs, side_tm, LANES), lambda b, h: (side_tile(b, h)[1], side_tile(b, h)[0], 0))
    dh = SB_HEAD_DIM
    blk = SB_BLOCK
    j_idx = np.arange(blk)[:, None]
    s_idx = np.arange(blk)[None, :]
    half = np.concatenate([(j_idx > s_idx).astype(np.float32), np.ones((blk, blk), np.float32)], axis=1)
    suffix01 = jnp.asarray(np.concatenate([half, half], axis=0), BF16)

    def head_spec(which):
        return pl.BlockSpec((s, dh), functools.partial(lambda b, h, which: (b, which * n_heads + h), which=which))

    out, side, *rider_out = pl.pallas_call(
        functools.partial(_sb_kernel, scale=float(1.0 / np.sqrt(dh).astype(np.float32)),
                          n_riders=len(riders)),
        grid=(bsz, n_heads),
        in_specs=[head_spec(0), head_spec(1), head_spec(2),
                  pl.BlockSpec((2 * blk, 2 * blk), lambda b, h: (0, 0))] + side_in + r_in,
        out_specs=[pl.BlockSpec((s, dh), lambda b, h: (b, h)), side_out] + r_out,
        out_shape=[jax.ShapeDtypeStruct((t, n_heads * dh), BF16),
                   jax.ShapeDtypeStruct((side_n // LANES, t, LANES), BF16)] + r_shapes,
        compiler_params=_params(("arbitrary", "arbitrary"), VMEM_LIMIT_LARGE),
        name="stick_breaking",
    )(qkv, qkv, qkv, suffix01, side_a, side_w_t, *[r.w for r in riders])
    return out, side, rider_out


def _layer(h, p, ffn1_pre_w, ffn1_w1, ffn1_w3, ffn1_w2, ffn1_post_w, mix_pre_w, w_in, conv_w, conv_b,
           dt_bias, a_log, d_skip, ssm_norm_w, w_ssm_out, w_sb_out, w_out, mix_post_w, ffn2_pre_w,
           ffn2_w1, ffn2_w3, ffn2_w2, ffn2_post_w, ple_w_gate, ple_w_proj, ple_norm_w, bsz):
    t, d = h.shape
    d_inner = w_ssm_out.shape[0]
    sb_width = w_sb_out.shape[0]
    n_ssm_heads = d_inner // SSM_HEAD_DIM
    conv_dim = conv_w.shape[1]
    bf = lambda w: w.astype(BF16)

    off_xbc = d_inner
    off_dt = off_xbc + conv_dim
    off_q = off_dt + n_ssm_heads
    off_gate = off_q + 3 * sb_width
    w_in_t = w_in.T
    (xn,) = _rowwise(_norm_fn, [h], [ffn1_pre_w], [BF16], "ffn1_pre_norm")
    f, (w_zx_t, w_qkv_t, w_gates_t) = _ffn(
        xn, bf(ffn1_w1), bf(ffn1_w3), bf(ffn1_w2), "ffn1",
        riders=[_Rider(w_in_t, 0, off_dt), _Rider(w_in_t, off_q, off_gate - off_q), _Rider(w_in_t, off_gate)])
    h, u = _rowwise(functools.partial(_residual_norm_fn, scale=0.5), [h, f], [ffn1_post_w, mix_pre_w],
                    [F32, BF16], "ffn1_residual_mix_norm")

    xbc, (ffn2_w1_bf,) = _mm([(u, w_zx_t)], BF16, "proj_xbc", _ep_identity, w_transposed=True,
                             w_row0=off_xbc, n=conv_dim, riders=[_Rider(ffn2_w1)])
    w_dt_t = jnp.zeros((LANES, d), BF16).at[:n_ssm_heads].set(bf(w_in_t[off_dt:off_q]))
    dt_raw, _ = _mm([(u, w_dt_t)], F32, "proj_dt", _ep_identity, w_transposed=True)
    qkv, (w_ssm_out_bf, w_sb_out_bf, w_out_bf, ple_w_gate_bf, ffn2_w2_bf) = _mm(
        [(u, w_qkv_t)], BF16, "proj_qkv", _ep_identity, w_transposed=True,
        riders=[_Rider(w_ssm_out), _Rider(w_sb_out), _Rider(w_out), _Rider(ple_w_gate), _Rider(ffn2_w2)])
    gates, (ffn2_w3_bf,) = _mm([(u, w_gates_t)], BF16, "proj_gates", _ep_sigmoid, w_transposed=True,
                               riders=[_Rider(ffn2_w3)])

    y_sb, z, _ = _stick_breaking(qkv, bsz, sb_width // SB_HEAD_DIM, side_a=u, side_w_t=w_zx_t, side_n=d_inner)
    y_ssm, sb_branch = _ssd(z, xbc, dt_raw, conv_w, conv_b, dt_bias, a_log, d_skip, ssm_norm_w, bsz,
                            side_a=y_sb, side_w=w_sb_out_bf, side_g=gates, side_g_col0=d)
    merged, _ = _mm([(y_ssm, w_ssm_out_bf)], BF16, "ssm_out_merge", _ep_gated_add,
                    extras=[(gates, 0), (sb_branch, 0)])
    mix, _ = _mm([(merged, w_out_bf)], BF16, "mix_out", _ep_identity)
    h, xn = _rowwise(functools.partial(_residual_norm_fn, scale=1.0), [h, mix], [mix_post_w, ffn2_pre_w],
                     [F32, BF16], "mix_residual_ffn2_norm")

    f, _ = _ffn(xn, ffn2_w1_bf, ffn2_w3_bf, ffn2_w2_bf, "ffn2")
    h, h_bf = _rowwise(functools.partial(_residual_cast_fn, scale=0.5), [h, f], [ffn2_post_w],
                       [F32, BF16], "ffn2_residual")

    ple, _ = _mm([(bf(p), bf(ple_w_proj)), (h_bf, ple_w_gate_bf)], BF16, "ple_gated", _ep_ple)
    (h,) = _rowwise(functools.partial(_residual_fn, scale=1.0), [h, ple], [ple_norm_w], [F32], "ple_residual")
    return h


def kernel(x, p, ffn1_pre_w, ffn1_w1, ffn1_w3, ffn1_w2, ffn1_post_w, mix_pre_w, w_in, conv_w, conv_b, dt_bias, a_log, d_skip, ssm_norm_w, w_ssm_out, w_sb_out, w_out, mix_post_w, ffn2_pre_w, ffn2_w1, ffn2_w3, ffn2_w2, ffn2_post_w, ple_w_gate, ple_w_proj, ple_norm_w):
    bsz, seqlen, d = x.shape
    depth = p.shape[0]
    h = x.reshape(bsz * seqlen, d)
    per_layer = (ffn1_pre_w, ffn1_w1, ffn1_w3, ffn1_w2, ffn1_post_w, mix_pre_w, w_in, conv_w, conv_b,
                 dt_bias, a_log, d_skip, ssm_norm_w, w_ssm_out, w_sb_out, w_out, mix_post_w, ffn2_pre_w,
                 ffn2_w1, ffn2_w3, ffn2_w2, ffn2_post_w, ple_w_gate, ple_w_proj, ple_norm_w)
    for i in range(depth):
        h = _layer(h, p[i].reshape(bsz * seqlen, -1), *[w[i] for w in per_layer], bsz=bsz)
    return h.reshape(bsz, seqlen, d)
```

```python
import functools
from typing import NamedTuple

import jax
import jax.numpy as jnp
import numpy as np
from jax.experimental import pallas as pl
from jax.experimental.pallas import tpu as pltpu

F32 = jnp.float32
BF16 = jnp.bfloat16

NORM_EPS = 1e-6
SSM_HEAD_DIM = 64
SSM_GROUPS = 8
SSM_D_STATE = 128
SSM_CONV = 4
SSM_CHUNK = 128
SSD_SIDE_ROW_CHUNKS = 4
SB_HEAD_DIM = 128
SB_BLOCK = 128
SB_WINDOW = 3
SB_GROUP = 8
LANES = 128
MXU_COLS = 256
BF16_SUBLANES = 16
CONV_PAD_ROWS = 8
VMEM_LIMIT = 56 * 1024 * 1024
VMEM_LIMIT_LARGE = 60 * 1024 * 1024
SB_ZERO_LOG = -105.0


def _params(semantics, vmem_limit=VMEM_LIMIT):
    return pltpu.CompilerParams(dimension_semantics=semantics, vmem_limit_bytes=vmem_limit)


def _split_bf16(x, parts):
    out = []
    rem = x
    for _ in range(parts):
        piece = rem.astype(BF16)
        out.append(piece)
        rem = rem - piece.astype(F32)
    return out


def _dot(a, b):
    return jnp.dot(a, b, preferred_element_type=F32)


def _dot_01(x, m01, parts, m01_on_left=False):
    acc = None
    for piece in _split_bf16(x, parts):
        term = _dot(m01, piece) if m01_on_left else _dot(piece, m01)
        acc = term if acc is None else acc + term
    return acc


def _softplus(x):
    return jnp.maximum(x, 0.0) + jnp.log1p(jnp.exp(-jnp.abs(x)))


def _softplus_abs(x):
    return jnp.maximum(x, 0.0) + jnp.log(1.0 + jnp.exp(-jnp.abs(x)))


def _silu(x):
    return x * jax.nn.sigmoid(x)


def _rms(x, w):
    x = x.astype(F32)
    return x * jax.lax.rsqrt(jnp.mean(x * x, axis=-1, keepdims=True) + NORM_EPS) * w


class _Rider(NamedTuple):
    w: jax.Array
    row0: int = 0
    rows: int | None = None

    def n_rows(self):
        return self.w.shape[0] - self.row0 if self.rows is None else self.rows


def _cast_row_block(row0, rows, steps):
    block = BF16_SUBLANES
    while rows % block or row0 % block or rows // block > steps:
        block += BF16_SUBLANES
        assert block <= rows, (row0, rows, steps)
    return block


def _rider_specs(riders, n_steps, step_of):
    in_specs, out_specs, out_shapes = [], [], []
    for rider in riders:
        rows, cols = rider.n_rows(), rider.w.shape[1]
        block = _cast_row_block(rider.row0, rows, n_steps)

        def index(*grid_idx, first=0, last=rows // block - 1):
            return first + jnp.minimum(step_of(*grid_idx), last), 0

        in_specs.append(pl.BlockSpec((block, cols), functools.partial(index, first=rider.row0 // block)))
        out_specs.append(pl.BlockSpec((block, cols), index))
        out_shapes.append(jax.ShapeDtypeStruct((rows, cols), BF16))
    return in_specs, out_specs, out_shapes


def _run_riders(in_refs, out_refs):
    for w_ref, wbf_ref in zip(in_refs, out_refs, strict=True):
        wbf_ref[...] = w_ref[...].astype(BF16)


def _rowwise_kernel(*refs, fn, n_rows, n_vecs):
    rows = [r[...] for r in refs[:n_rows]]
    vecs = [r[...] for r in refs[n_rows:n_rows + n_vecs]]
    outs = fn(rows, vecs)
    for o_ref, o in zip(refs[n_rows + n_vecs:], outs, strict=True):
        o_ref[...] = o.astype(o_ref.dtype)


def _rowwise(fn, rows, vecs, out_dtypes, name, tm=256):
    t, d = rows[0].shape
    tm = min(tm, t)
    row_spec = pl.BlockSpec((tm, d), lambda i: (i, 0))
    vec_spec = pl.BlockSpec((1, d), lambda i: (0, 0))
    return pl.pallas_call(
        functools.partial(_rowwise_kernel, fn=fn, n_rows=len(rows), n_vecs=len(vecs)),
        grid=(t // tm,),
        in_specs=[row_spec] * len(rows) + [vec_spec] * len(vecs),
        out_specs=[row_spec] * len(out_dtypes),
        out_shape=[jax.ShapeDtypeStruct((t, d), dt) for dt in out_dtypes],
        compiler_params=_params(("parallel",)),
        name=name,
    )(*rows, *[v.reshape(1, d) for v in vecs])


def _norm_fn(rows, vecs):
    return (_rms(rows[0], vecs[0]),)


def _residual_norm_fn(rows, vecs, *, scale):
    h = rows[0] + scale * _rms(rows[1], vecs[0])
    return h, _rms(h, vecs[1])


def _residual_cast_fn(rows, vecs, *, scale):
    h = rows[0] + scale * _rms(rows[1], vecs[0])
    return h, h


def _residual_fn(rows, vecs, *, scale):
    return (rows[0] + scale * _rms(rows[1], vecs[0]),)


def _ffn_kernel(xn_ref, w1_ref, w3_ref, w2_ref, *refs, n_riders, up_transposed):
    rider_in, o_ref, rider_out, acc_ref = refs[:n_riders], refs[n_riders], refs[n_riders + 1:-1], refs[-1]

    @pl.when(pl.program_id(1) == 0)
    def _():
        acc_ref[...] = jnp.zeros_like(acc_ref)

    _run_riders(rider_in, rider_out)
    x = xn_ref[...]
    contract = (((1,), (1,)), ((), ())) if up_transposed else (((1,), (0,)), ((), ()))
    h1 = jax.lax.dot_general(x, w1_ref[...], contract, preferred_element_type=F32)
    h3 = jax.lax.dot_general(x, w3_ref[...], contract, preferred_element_type=F32)
    g = (_silu(h1) * h3).astype(BF16)
    acc_ref[...] += _dot(g, w2_ref[...])

    @pl.when(pl.program_id(1) == pl.num_programs(1) - 1)
    def _():
        o_ref[...] = acc_ref[...].astype(o_ref.dtype)


def _ffn(xn, w1, w3, w2, name, riders=(), up_transposed=False, tm=1024, tf=256):
    t, d = xn.shape
    f = w2.shape[0]
    tm = min(tm, t)
    grid = (t // tm, f // tf)
    r_in, r_out, r_shapes = _rider_specs(riders, grid[0] * grid[1], lambda i, j: i * grid[1] + j)
    up_spec = pl.BlockSpec((tf, d), lambda i, j: (j, 0)) if up_transposed else pl.BlockSpec((d, tf), lambda i, j: (0, j))
    out, *rider_out = pl.pallas_call(
        functools.partial(_ffn_kernel, n_riders=len(riders), up_transposed=up_transposed),
        grid=grid,
        in_specs=[
            pl.BlockSpec((tm, d), lambda i, j: (i, 0)),
            up_spec,
            up_spec,
            pl.BlockSpec((tf, d), lambda i, j: (j, 0)),
        ] + r_in,
        out_specs=[pl.BlockSpec((tm, d), lambda i, j: (i, 0), pipeline_mode=pl.Buffered(1))] + r_out,
        out_shape=[jax.ShapeDtypeStruct((t, d), BF16)] + r_shapes,
        scratch_shapes=[pltpu.VMEM((tm, d), F32)],
        compiler_params=_params(("arbitrary", "arbitrary"), VMEM_LIMIT_LARGE),
        name=name,
    )(xn, w1, w3, w2, *[r.w for r in riders])
    return out, rider_out


def _mm_kernel(*refs, n_pairs, n_extras, n_riders, epilogue, w_transposed):
    a_refs = refs[:n_pairs]
    w_refs = refs[n_pairs:2 * n_pairs]
    e_refs = refs[2 * n_pairs:2 * n_pairs + n_extras]
    rest = refs[2 * n_pairs + n_extras:]
    rider_in, o_ref, rider_out = rest[:n_riders], rest[n_riders], rest[n_riders + 1:]
    _run_riders(rider_in, rider_out)
    contract = (((1,), (1,)), ((), ())) if w_transposed else (((1,), (0,)), ((), ()))
    accs = [jax.lax.dot_general(a[...], w[...], contract, preferred_element_type=F32)
            for a, w in zip(a_refs, w_refs, strict=True)]
    o_ref[...] = epilogue(accs, [e[...] for e in e_refs]).astype(o_ref.dtype)


def _mm(pairs, out_dtype, name, epilogue, extras=(), riders=(), w_transposed=False, w_row0=0, n=None,
        tm=1024, tn=1024, vmem_limit=VMEM_LIMIT):
    t = pairs[0][0].shape[0]
    if n is None:
        n = pairs[0][1].shape[0 if w_transposed else 1] - w_row0
    tn = min(tn, n)
    tm = min(tm, t)
    assert n % tn == 0 and w_row0 % tn == 0 and (w_transposed or not w_row0)
    grid = (t // tm, n // tn)
    a_specs = [pl.BlockSpec((tm, a.shape[1]), lambda i, j: (i, 0)) for a, _ in pairs]
    if w_transposed:
        w_specs = [pl.BlockSpec((tn, w.shape[1]), lambda i, j: (j + w_row0 // tn, 0)) for _, w in pairs]
    else:
        w_specs = [pl.BlockSpec((w.shape[0], tn), lambda i, j: (0, j)) for _, w in pairs]
    e_specs = [pl.BlockSpec((tm, tn), functools.partial(lambda i, j, off: (i, j + off), off=off))
               for _, off in extras]
    r_in, r_out, r_shapes = _rider_specs(riders, grid[0] * grid[1], lambda i, j: i * grid[1] + j)
    out, *rider_out = pl.pallas_call(
        functools.partial(_mm_kernel, n_pairs=len(pairs), n_extras=len(extras), n_riders=len(riders),
                          epilogue=epilogue, w_transposed=w_transposed),
        grid=grid,
        in_specs=a_specs + w_specs + e_specs + r_in,
        out_specs=[pl.BlockSpec((tm, tn), lambda i, j: (i, j))] + r_out,
        out_shape=[jax.ShapeDtypeStruct((t, n), out_dtype)] + r_shapes,
        compiler_params=_params(("arbitrary", "arbitrary"), vmem_limit),
        name=name,
    )(*[a for a, _ in pairs], *[w for _, w in pairs], *[e for e, _ in extras], *[r.w for r in riders])
    return out, rider_out


def _ep_identity(accs, extras):
    return accs[0]


def _ep_gated_add(accs, extras):
    return extras[0].astype(F32) * accs[0] + extras[1].astype(F32)


def _ep_sigmoid(accs, extras):
    return jax.nn.sigmoid(accs[0])


def _ep_ple(accs, extras):
    return accs[0] * jax.nn.sigmoid(accs[1])


def _ssd_kernel(z_ref, xbc_ref, dt_ref, convw_ref, convb_ref, dtb_ref, alog_ref, dskip_ref, normw_ref,
                tri_ref, expand_ref, side_a_ref, side_w_ref, side_g_ref, o_ref, side_o_ref,
                ext_ref, state_ref, y_ref, *, d_inner):
    l = SSM_CHUNK
    n = SSM_D_STATE
    gw = d_inner // SSM_GROUPS
    heads_per_group = gw // SSM_HEAD_DIM
    off_b = d_inner
    off_c = d_inner + SSM_GROUPS * n

    @pl.when(pl.program_id(1) == 0)
    def _():
        state_ref[...] = jnp.zeros_like(state_ref)
        ext_ref[0:CONV_PAD_ROWS, :] = jnp.zeros((CONV_PAD_ROWS, ext_ref.shape[1]), F32)

    ext_ref[CONV_PAD_ROWS:CONV_PAD_ROWS + l, :] = xbc_ref[...].astype(F32)

    def conv_silu(start, width):
        cols = slice(start, start + width)
        acc = convb_ref[:, cols]
        for k in range(SSM_CONV):
            row0 = CONV_PAD_ROWS - (SSM_CONV - 1) + k
            acc = acc + convw_ref[k:k + 1, cols] * ext_ref[row0:row0 + l, cols]
        return _silu(acc)

    dt = _softplus(dt_ref[...] + dtb_ref[...])
    da = dt * (-jnp.exp(alog_ref[...]))
    a_cum = _dot_01(da, tri_ref[...], 3, m01_on_left=True)
    a_cum_t = a_cum.T
    out_decay = jnp.exp(a_cum)
    state_decay = jnp.exp(a_cum[l - 1:l, :] - a_cum)
    pieces = [p for x in (dt, out_decay, state_decay) for p in _split_bf16(x, 2)]
    expanded = _dot(jnp.concatenate(pieces, axis=0), expand_ref[...])
    dt_full, out_decay_full, state_decay_full = (
        expanded[2 * k * l:(2 * k + 1) * l] + expanded[(2 * k + 1) * l:(2 * k + 2) * l] for k in range(3))

    row = jax.lax.broadcasted_iota(jnp.int32, (l, l), 0)
    col = jax.lax.broadcasted_iota(jnp.int32, (l, l), 1)
    causal = col <= row
    lane = jax.lax.broadcasted_iota(jnp.int32, (l, 2 * SSM_HEAD_DIM), 1)
    first_head = lane < SSM_HEAD_DIM

    side_slabs = side_o_ref.shape[1] // MXU_COLS

    for g in range(SSM_GROUPS):
        if g % (SSM_GROUPS // side_slabs) == 0:
            slab = g // (SSM_GROUPS // side_slabs)
            sc = slice(slab * MXU_COLS, (slab + 1) * MXU_COLS)
            side_o_ref[:, sc] = (side_g_ref[:, sc].astype(F32) * _dot(side_a_ref[...], side_w_ref[:, sc])
                                 ).astype(side_o_ref.dtype)
        ch = slice(g * gw, (g + 1) * gw)
        xs = conv_silu(g * gw, gw)
        bm = conv_silu(off_b + g * n, n)
        cm = conv_silu(off_c + g * n, n).astype(BF16)
        bm_t = bm.T.astype(BF16)
        xdt = xs * dt_full[:, ch]
        cb = _dot(cm, bm_t)

        y_parts = []
        for pair in range(heads_per_group // 2):
            gs = []
            for r in (2 * pair, 2 * pair + 1):
                h = g * heads_per_group + r
                diff = a_cum[:, h:h + 1] - a_cum_t[h:h + 1, :]
                decay = jnp.exp(jnp.where(causal, diff, -1e30))
                gs.append((cb * decay).astype(BF16))
            x_pair = xdt[:, pair * 2 * SSM_HEAD_DIM:(pair + 1) * 2 * SSM_HEAD_DIM]
            block_diag = jnp.concatenate(
                [jnp.where(first_head, x_pair, 0.0).astype(BF16),
                 jnp.where(first_head, 0.0, x_pair).astype(BF16)], axis=0)
            y_parts.append(_dot(jnp.concatenate(gs, axis=1), block_diag))
        y_diag = jnp.concatenate(y_parts, axis=1)

        prev = state_ref[:, ch]
        y_off = _dot(cm, prev.astype(BF16)) * out_decay_full[:, ch]
        new_states = _dot(bm_t, (xdt * state_decay_full[:, ch]).astype(BF16))
        state_ref[:, ch] = prev * out_decay_full[l - 1:l, ch] + new_states

        y = y_diag + y_off + dskip_ref[:, ch] * xs
        z = jnp.concatenate([z_ref[g * (gw // LANES) + m] for m in range(gw // LANES)], axis=1).astype(F32)
        y_ref[:, ch] = y * _silu(z)

    ext_ref[0:CONV_PAD_ROWS, :] = ext_ref[l:l + CONV_PAD_ROWS, :]
    o_ref[...] = _rms(y_ref[...], normw_ref[...]).astype(o_ref.dtype)


def _ssd(z, xbc, dt_raw, conv_w, conv_b, dt_bias, a_log, d_skip, norm_w, bsz, side_a, side_w, side_g,
         side_g_col0):
    side_tm = SSD_SIDE_ROW_CHUNKS * SSM_CHUNK
    side_k, side_n = side_w.shape
    side_tn = side_n // SSD_SIDE_ROW_CHUNKS
    side_g_off = side_g_col0 // side_tn

    def side_tile(b, c):
        step = b * nc + c
        return step // SSD_SIDE_ROW_CHUNKS, step % SSD_SIDE_ROW_CHUNKS

    t, conv_dim = xbc.shape
    d_inner = z.shape[0] * LANES
    n_heads = d_inner // SSM_HEAD_DIM
    nc = t // bsz // SSM_CHUNK
    l = SSM_CHUNK

    def pad_heads(v):
        return jnp.zeros((1, LANES), F32).at[0, :n_heads].set(v.astype(F32))

    tri = jnp.asarray(np.tril(np.ones((l, l), np.float32)), BF16)
    expand = np.zeros((LANES, d_inner), np.float32)
    expand[np.arange(d_inner) // SSM_HEAD_DIM, np.arange(d_inner)] = 1.0
    expand = jnp.asarray(expand, BF16)
    d_skip_full = jnp.repeat(d_skip.astype(F32), SSM_HEAD_DIM).reshape(1, d_inner)

    def whole(shape):
        return pl.BlockSpec(shape, lambda b, c: (0, 0))

    return pl.pallas_call(
        functools.partial(_ssd_kernel, d_inner=d_inner),
        grid=(bsz, nc),
        in_specs=[
            pl.BlockSpec((d_inner // LANES, l, LANES), lambda b, c: (0, b * nc + c, 0)),
            pl.BlockSpec((l, conv_dim), lambda b, c: (b * nc + c, 0)),
            pl.BlockSpec((l, LANES), lambda b, c: (b * nc + c, 0)),
            whole((SSM_CONV, conv_dim)), whole((1, conv_dim)), whole((1, LANES)), whole((1, LANES)),
            whole((1, d_inner)), whole((1, d_inner)), whole((l, l)), whole((LANES, d_inner)),
            pl.BlockSpec((side_tm, side_k), lambda b, c: (side_tile(b, c)[0], 0)),
            pl.BlockSpec((side_k, side_tn), lambda b, c: (0, side_tile(b, c)[1])),
            pl.BlockSpec((side_tm, side_tn), lambda b, c: (side_tile(b, c)[0], side_tile(b, c)[1] + side_g_off)),
        ],
        out_specs=[pl.BlockSpec((l, d_inner), lambda b, c: (b * nc + c, 0)),
                   pl.BlockSpec((side_tm, side_tn), side_tile)],
        out_shape=[jax.ShapeDtypeStruct((t, d_inner), BF16), jax.ShapeDtypeStruct((t, side_n), BF16)],
        scratch_shapes=[
            pltpu.VMEM((CONV_PAD_ROWS + l, conv_dim), F32),
            pltpu.VMEM((SSM_D_STATE, d_inner), F32),
            pltpu.VMEM((l, d_inner), F32),
        ],
        compiler_params=_params(("arbitrary", "arbitrary"), VMEM_LIMIT_LARGE),
        name="ssd_mixer",
    )(z, xbc, dt_raw, conv_w.astype(F32), conv_b.astype(F32).reshape(1, conv_dim), pad_heads(dt_bias),
      pad_heads(a_log), d_skip_full, norm_w.astype(F32).reshape(1, d_inner), tri, expand,
      side_a, side_w, side_g)


def _sb_kernel(q_ref, k_ref, v_ref, suffix_ref, side_a_ref, side_w_ref, *refs, scale, n_riders):
    rider_in, o_ref, side_o_ref, rider_out = refs[:n_riders], refs[n_riders], refs[n_riders + 1], refs[n_riders + 2:]
    _run_riders(rider_in, rider_out)
    side_half = side_a_ref.shape[0] // 2

    def side_piece(k):
        def aligned(start, unit):
            return start if isinstance(start, int) else pl.multiple_of(start, unit)

        rows = pl.ds(aligned((k % 2) * side_half, side_half), side_half)
        cols = pl.ds(aligned((k // 2) * MXU_COLS, MXU_COLS), MXU_COLS)
        piece = jax.lax.dot_general(side_a_ref[rows, :], side_w_ref[cols, :],
                                    (((1,), (1,)), ((), ())), preferred_element_type=F32)
        for m in range(MXU_COLS // LANES):
            side_o_ref[(MXU_COLS // LANES) * (k // 2) + m, rows, :] = (
                piece[:, m * LANES:(m + 1) * LANES].astype(side_o_ref.dtype))

    blk = SB_BLOCK
    nq = q_ref.shape[0] // blk
    suffix01 = suffix_ref[...]
    row = jax.lax.broadcasted_iota(jnp.int32, (blk, blk), 0)
    col = jax.lax.broadcasted_iota(jnp.int32, (blk, blk), 1)
    strictly_causal = col < row

    def blocks(first, count=1):
        start = first * blk
        if not isinstance(start, int):
            start = pl.multiple_of(start, blk)
        return pl.ds(start, count * blk)

    def visit(q, j, carry, acc, mask):
        ks = blocks(j)
        z = jax.lax.dot_general(q, k_ref[ks, :], (((1,), (1,)), ((), ())),
                                preferred_element_type=F32) * scale
        sp = _softplus_abs(z)
        log_beta = z - sp
        log_1mb = -sp
        if mask is not None:
            log_1mb = jnp.where(mask, log_1mb, 0.0)
        hi, lo = _split_bf16(log_1mb, 2)
        sums = _dot(jnp.concatenate([hi, lo], axis=1), suffix01)
        w = jnp.exp(log_beta + sums[:, :blk] + carry)
        if mask is not None:
            w = jnp.where(mask, w, 0.0)
        acc = acc + _dot(w.astype(BF16), v_ref[ks, :])
        return carry + sums[:, blk:], acc

    def q_slice(i):
        return blocks(i)

    def finish(i, q, j, carry, acc):
        def cond(state):
            j, carry, _ = state
            return jnp.logical_and(j >= 0, jnp.max(carry) > SB_ZERO_LOG)

        def body(state):
            j, carry, acc = state
            carry, acc = visit(q, j, carry, acc, None)
            return j - 1, carry, acc

        _, _, acc = jax.lax.while_loop(cond, body, (jnp.int32(j), carry, acc))
        o_ref[q_slice(i), :] = acc.astype(o_ref.dtype)

    def single(i):
        q = q_ref[q_slice(i), :]
        zeros = jnp.zeros((blk, blk), F32)
        carry, acc = visit(q, i, zeros, zeros, strictly_causal)
        for j in reversed(range(i)):
            carry, acc = visit(q, j, carry, acc, None)
        o_ref[q_slice(i), :] = acc.astype(o_ref.dtype)

    last = (SB_WINDOW - 1) * blk

    def window_scores(q, ks):
        z = jax.lax.dot_general(q, k_ref[ks, :], (((1,), (1,)), ((), ())),
                                preferred_element_type=F32) * scale
        sp = _softplus_abs(z)
        log_1mb = jnp.concatenate([-sp[:, :last], jnp.where(strictly_causal, -sp[:, last:], 0.0)], axis=1)
        hi, lo = _split_bf16(log_1mb, 2)
        stacked = jnp.concatenate(
            [jnp.concatenate([hi[:, b * blk:(b + 1) * blk], lo[:, b * blk:(b + 1) * blk]], axis=1)
             for b in range(SB_WINDOW)], axis=0)
        return z - sp, stacked

    def window_weights(log_beta, sums):
        later = jnp.zeros((blk, blk), F32)
        excl = [None] * SB_WINDOW
        for b in reversed(range(SB_WINDOW)):
            rows = slice(b * blk, (b + 1) * blk)
            excl[b] = sums[rows, :blk] + later
            later = later + sums[rows, blk:]
        w = jnp.exp(log_beta + jnp.concatenate(excl, axis=1))
        w = jnp.concatenate([w[:, :last], jnp.where(strictly_causal, w[:, last:], 0.0)], axis=1)
        return w.astype(BF16), later

    def group(first, size, piece):
        side_piece(piece)
        idx = [first + g for g in range(size)]
        qs = [q_ref[q_slice(i), :] for i in idx]
        kss = [blocks(i - (SB_WINDOW - 1), SB_WINDOW) for i in idx]
        scored = [window_scores(q, ks) for q, ks in zip(qs, kss)]
        sums = [_dot(stacked, suffix01) for _, stacked in scored]
        weighted = [window_weights(log_beta, s) for (log_beta, _), s in zip(scored, sums)]
        accs = [_dot(w, v_ref[ks, :]) for (w, _), ks in zip(weighted, kss)]
        for i, acc in zip(idx, accs):
            o_ref[q_slice(i), :] = acc.astype(o_ref.dtype)
        carries = [carry for _, carry in weighted]

        @pl.when(jnp.max(functools.reduce(jnp.maximum, carries)) > SB_ZERO_LOG)
        def _():
            for i, q, carry, acc in zip(idx, qs, carries, accs):
                finish(i, q, i - SB_WINDOW, carry, acc)

    def full_group(ig, _):
        group(n_single + SB_GROUP * ig, SB_GROUP, ig)
        return 0

    n_single, n_groups, n_left = _sb_schedule(nq)
    n_pieces = n_groups + (1 if n_left else 0)
    assert side_o_ref.shape[0] == n_pieces and n_pieces % 2 == 0
    for i in range(n_single):
        single(i)
    jax.lax.fori_loop(0, n_groups, full_group, 0)
    if n_left:
        group(nq - n_left, n_left, n_groups)


def _sb_schedule(nq):
    n_single = min(SB_WINDOW - 1, nq)
    return (n_single, *divmod(nq - n_single, SB_GROUP))


def _stick_breaking(qkv, bsz, n_heads, side_a, side_w_t, side_n, riders=(), side_tm=1024):
    r_in, r_out, r_shapes = _rider_specs(riders, bsz * n_heads, lambda b, h: b * n_heads + h)
    t = qkv.shape[0]
    s = t // bsz
    _, n_groups, n_left = _sb_schedule(s // SB_BLOCK)
    slabs = n_groups + (1 if n_left else 0)
    side_tm = min(side_tm, t)
    side_k = side_a.shape[1]
    n_col_tiles = side_n // (slabs * LANES)
    assert side_n % (slabs * LANES) == 0 and (t // side_tm) * n_col_tiles == bsz * n_heads, (t, side_n, slabs)

    def side_tile(b, h):
        step = b * n_heads + h
        return step // n_col_tiles, step % n_col_tiles

    side_in = [pl.BlockSpec((side_tm, side_k), lambda b, h: (side_tile(b, h)[0], 0)),
               pl.BlockSpec((slabs * LANES, side_k), lambda b, h: (side_tile(b, h)[1], 0))]
    side_out = pl.BlockSpec((slabs, side_tm, LANES), lambda b, h: (side_tile(b, h)[1], side_tile(b, h)[0], 0))
    dh = SB_HEAD_DIM
    blk = SB_BLOCK
    j_idx = np.arange(blk)[:, None]
    s_idx = np.arange(blk)[None, :]
    half = np.concatenate([(j_idx > s_idx).astype(np.float32), np.ones((blk, blk), np.float32)], axis=1)
    suffix01 = jnp.asarray(np.concatenate([half, half], axis=0), BF16)

    def head_spec(which):
        return pl.BlockSpec((s, dh), functools.partial(lambda b, h, which: (b, which * n_heads + h), which=which))

    out, side, *rider_out = pl.pallas_call(
        functools.partial(_sb_kernel, scale=float(1.0 / np.sqrt(dh).astype(np.float32)),
                          n_riders=len(riders)),
        grid=(bsz, n_heads),
        in_specs=[head_spec(0), head_spec(1), head_spec(2),
                  pl.BlockSpec((2 * blk, 2 * blk), lambda b, h: (0, 0))] + side_in + r_in,
        out_specs=[pl.BlockSpec((s, dh), lambda b, h: (b, h)), side_out] + r_out,
        out_shape=[jax.ShapeDtypeStruct((t, n_heads * dh), BF16),
                   jax.ShapeDtypeStruct((side_n // LANES, t, LANES), BF16)] + r_shapes,
        compiler_params=_params(("arbitrary", "arbitrary"), VMEM_LIMIT_LARGE),
        name="stick_breaking",
    )(qkv, qkv, qkv, suffix01, side_a, side_w_t, *[r.w for r in riders])
    return out, side, rider_out


def _layer(h, p, ffn1_pre_w, ffn1_w1, ffn1_w3, ffn1_w2, ffn1_post_w, mix_pre_w, w_in, conv_w, conv_b,
           dt_bias, a_log, d_skip, ssm_norm_w, w_ssm_out, w_sb_out, w_out, mix_post_w, ffn2_pre_w,
           ffn2_w1, ffn2_w3, ffn2_w2, ffn2_post_w, ple_w_gate, ple_w_proj, ple_norm_w, bsz):
    t, d = h.shape
    d_inner = w_ssm_out.shape[0]
    sb_width = w_sb_out.shape[0]
    n_ssm_heads = d_inner // SSM_HEAD_DIM
    conv_dim = conv_w.shape[1]
    bf = lambda w: w.astype(BF16)

    off_xbc = d_inner
    off_dt = off_xbc + conv_dim
    off_q = off_dt + n_ssm_heads
    off_gate = off_q + 3 * sb_width
    w_in_t = w_in.T
    (xn,) = _rowwise(_norm_fn, [h], [ffn1_pre_w], [BF16], "ffn1_pre_norm")
    f, (w_zx_t, w_qkv_t, w_gates_t) = _ffn(
        xn, bf(ffn1_w1).T, bf(ffn1_w3).T, bf(ffn1_w2), "ffn1", up_transposed=True,
        riders=[_Rider(w_in_t, 0, off_dt), _Rider(w_in_t, off_q, off_gate - off_q), _Rider(w_in_t, off_gate)])
    h, u = _rowwise(functools.partial(_residual_norm_fn, scale=0.5), [h, f], [ffn1_post_w, mix_pre_w],
                    [F32, BF16], "ffn1_residual_mix_norm")

    xbc, (ffn2_w1_bf,) = _mm([(u, w_zx_t)], BF16, "proj_xbc", _ep_identity, w_transposed=True,
                             w_row0=off_xbc, n=conv_dim, riders=[_Rider(ffn2_w1)])
    w_dt_t = jnp.zeros((LANES, d), BF16).at[:n_ssm_heads].set(bf(w_in_t[off_dt:off_q]))
    dt_raw, _ = _mm([(u, w_dt_t)], F32, "proj_dt", _ep_identity, w_transposed=True)
    qkv, (w_ssm_out_bf, w_sb_out_bf, w_out_bf, ple_w_gate_bf, ffn2_w2_bf) = _mm(
        [(u, w_qkv_t)], BF16, "proj_qkv", _ep_identity, w_transposed=True,
        riders=[_Rider(w_ssm_out), _Rider(w_sb_out), _Rider(w_out), _Rider(ple_w_gate), _Rider(ffn2_w2)])
    gates, (ffn2_w3_bf,) = _mm([(u, w_gates_t)], BF16, "proj_gates", _ep_sigmoid, w_transposed=True,
                               riders=[_Rider(ffn2_w3)])

    y_sb, z, _ = _stick_breaking(qkv, bsz, sb_width // SB_HEAD_DIM, side_a=u, side_w_t=w_zx_t, side_n=d_inner)
    y_ssm, sb_branch = _ssd(z, xbc, dt_raw, conv_w, conv_b, dt_bias, a_log, d_skip, ssm_norm_w, bsz,
                            side_a=y_sb, side_w=w_sb_out_bf, side_g=gates, side_g_col0=d)
    merged, _ = _mm([(y_ssm, w_ssm_out_bf)], BF16, "ssm_out_merge", _ep_gated_add,
                    extras=[(gates, 0), (sb_branch, 0)])
    mix, _ = _mm([(merged, w_out_bf)], BF16, "mix_out", _ep_identity)
    h, xn = _rowwise(functools.partial(_residual_norm_fn, scale=1.0), [h, mix], [mix_post_w, ffn2_pre_w],
                     [F32, BF16], "mix_residual_ffn2_norm")

    f, _ = _ffn(xn, ffn2_w1_bf, ffn2_w3_bf, ffn2_w2_bf, "ffn2")
    h, h_bf = _rowwise(functools.partial(_residual_cast_fn, scale=0.5), [h, f], [ffn2_post_w],
                       [F32, BF16], "ffn2_residual")

    ple, _ = _mm([(bf(p), bf(ple_w_proj)), (h_bf, ple_w_gate_bf)], BF16, "ple_gated", _ep_ple)
    (h,) = _rowwise(functools.partial(_residual_fn, scale=1.0), [h, ple], [ple_norm_w], [F32], "ple_residual")
    return h


def kernel(x, p, ffn1_pre_w, ffn1_w1, ffn1_w3, ffn1_w2, ffn1_post_w, mix_pre_w, w_in, conv_w, conv_b, dt_bias, a_log, d_skip, ssm_norm_w, w_ssm_out, w_sb_out, w_out, mix_post_w, ffn2_pre_w, ffn2_w1, ffn2_w3, ffn2_w2, ffn2_post_w, ple_w_gate, ple_w_proj, ple_norm_w):
    bsz, seqlen, d = x.shape
    depth = p.shape[0]
    h = x.reshape(bsz * seqlen, d)
    per_layer = (ffn1_pre_w, ffn1_w1, ffn1_w3, ffn1_w2, ffn1_post_w, mix_pre_w, w_in, conv_w, conv_b,
                 dt_bias, a_log, d_skip, ssm_norm_w, w_ssm_out, w_sb_out, w_out, mix_post_w, ffn2_pre_w,
                 ffn2_w1, ffn2_w3, ffn2_w2, ffn2_post_w, ple_w_gate, ple_w_proj, ple_norm_w)
    for i in range(depth):
        h = _layer(h, p[i].reshape(bsz * seqlen, -1), *[w[i] for w in per_layer], bsz=bsz)
    return h.reshape(bsz, seqlen, d)
```
